```python
import math
import jax, jax.numpy as jnp
from jax import lax
import numpy as np

D_MODEL = 1024
BATCH = 4
SEQ = 8192
DEPTH = 4
DEC_BATCH = 8
DEC_SEQ = 16
PAST_LEN = 2048

CHUNK = 64
N_META = 16
HEAD_DIM = 64
N_HEADS_A = 8
N_KV_A = 1
GROUP_A = N_HEADS_A // N_KV_A
N_IDX_HEADS = 4
IDX_DIM = 64
TOPK_MAX = 256
N_HEADS_SB = 4
SB_HEAD_DIM = 128
N_BUCKETS = 32
MAX_DISTANCE = 128
D_FF = 2816
CONV_W = 3
Q_BLOCK = 128
K_BLOCK = 128
FAR_POS = 2 ** 30
EPS = 1e-6

W_A = N_HEADS_A * HEAD_DIM
W_KV_A = N_KV_A * HEAD_DIM
W_IDX_Q = N_IDX_HEADS * IDX_DIM
W_SB = N_HEADS_SB * SB_HEAD_DIM
IN_COLS = W_A + 2 * W_KV_A + W_IDX_Q + IDX_DIM + N_IDX_HEADS + 3 * W_SB + 2 * D_MODEL

kernel_name = "dsa_stickbreak_gated_streaming_encoder_step"


def rms_norm(x, g):
    xf = x.astype(jnp.float32)
    y = xf * lax.rsqrt(jnp.mean(xf * xf, axis=-1, keepdims=True) + EPS)
    return (y * g.astype(jnp.float32)).astype(x.dtype)


def t5_bucket(rel):
    half = N_BUCKETS // 2
    max_exact = half // 2
    ret = jnp.where(rel > 0, half, 0)
    n = jnp.abs(rel)
    nf = jnp.maximum(n, 1).astype(jnp.float32)
    large = max_exact + (jnp.log(nf / max_exact) / math.log(MAX_DISTANCE / max_exact)
                         * (half - max_exact)).astype(jnp.int32)
    large = jnp.minimum(large, half - 1)
    return ret + jnp.where(n < max_exact, n, large)


def mixer_projections(xn, w_in_l, qn_g, kn_g):
    sizes = (W_A, W_KV_A, W_KV_A, W_IDX_Q, IDX_DIM, N_IDX_HEADS, W_SB, W_SB, W_SB, D_MODEL)
    offsets = []
    acc = 0
    for s in sizes:
        acc += s
        offsets.append(acc)
    q_a, k_a, v_a, q_i, k_i, w_i, q_s, k_s, v_s, g_a, g_s = jnp.split(xn @ w_in_l, offsets, axis=-1)
    b, n = xn.shape[:2]
    q_a = rms_norm(q_a.reshape(b, n, N_HEADS_A, HEAD_DIM), qn_g)
    k_a = rms_norm(k_a.reshape(b, n, N_KV_A, HEAD_DIM), kn_g)
    v_a = v_a.reshape(b, n, N_KV_A, HEAD_DIM)
    q_i = q_i.reshape(b, n, N_IDX_HEADS, IDX_DIM)
    q_s = q_s.reshape(b, n, N_HEADS_SB, SB_HEAD_DIM)
    k_s = k_s.reshape(b, n, N_HEADS_SB, SB_HEAD_DIM)
    v_s = v_s.reshape(b, n, N_HEADS_SB, SB_HEAD_DIM)
    return q_a, k_a, v_a, q_i, k_i, w_i, q_s, k_s, v_s, g_a, g_s


def dsa_attend(q_a, q_i, w_i, qpos, qchunk, k_a, v_a, k_i, kpos, kchunk, rel_bias, topk):
    b, nq = q_a.shape[:2]
    s = jnp.einsum('bqhd,bkd->bqhk', q_i, k_i).astype(jnp.float32) * (IDX_DIM ** -0.5)
    score = jnp.einsum('bqhk,bqh->bqk', jax.nn.relu(s), w_i.astype(jnp.float32)) * (N_IDX_HEADS ** -0.5)
    admissible = kchunk[None, :] <= qchunk[:, None]
    score = jnp.where(admissible[None], score, -jnp.inf)
    top_val, top_idx = lax.top_k(score, topk)
    valid = jnp.isfinite(top_val)
    gather = jax.vmap(lambda arr, idx: arr[idx])
    k_sel = gather(k_a, top_idx)
    v_sel = gather(v_a, top_idx)
    qg = q_a.reshape(b, nq, N_KV_A, GROUP_A, HEAD_DIM)
    logits = jnp.einsum('bqgrd,bqkgd->bqgrk', qg, k_sel).astype(jnp.float32) * (HEAD_DIM ** -0.5)
    rel = kpos[top_idx] - qpos[None, :, None]
    bias = rel_bias[t5_bucket(rel)].astype(jnp.float32)
    bias = bias.reshape(b, nq, topk, N_KV_A, GROUP_A).transpose(0, 1, 3, 4, 2)
    logits = jnp.where(valid[:, :, None, None, :], logits + bias, -jnp.inf)
    p = jax.nn.softmax(logits, axis=-1)
    out = jnp.einsum('bqgrk,bqkgd->bqgrd', p.astype(v_a.dtype), v_sel)
    return out.reshape(b, nq, W_A)


def sb_attend(q, k, v, qpos, kpos):
    b, nq = q.shape[:2]
    lk_len = k.shape[1]
    nkb = -(-lk_len // K_BLOCK)
    pad = nkb * K_BLOCK - lk_len
    k = jnp.pad(k, ((0, 0), (0, pad), (0, 0), (0, 0)))
    v = jnp.pad(v, ((0, 0), (0, pad), (0, 0), (0, 0)))
    kpos = jnp.pad(kpos, (0, pad), constant_values=FAR_POS)
    z = jnp.einsum('bqhd,bkhd->bhqk', q, k).astype(jnp.float32) * (SB_HEAD_DIM ** -0.5)
    before = kpos[None, :] < qpos[:, None]
    log_keep = jnp.where(before, jax.nn.log_sigmoid(-z), 0.0)
    lk = log_keep.reshape(b, N_HEADS_SB, nq, nkb, K_BLOCK)
    blk_sum = jnp.sum(lk, axis=-1)
    later_blocks = lax.cumsum(blk_sum, axis=3, reverse=True) - blk_sum
    tri = (jnp.arange(K_BLOCK)[:, None] > jnp.arange(K_BLOCK)[None, :]).astype(jnp.float32)
    within = jnp.einsum('bhqcj,js->bhqcs', lk, tri)
    later = (within + later_blocks[..., None]).reshape(b, N_HEADS_SB, nq, nkb * K_BLOCK)
    a = jnp.where(before, jnp.exp(log_keep + z + later), 0.0)
    out = jnp.einsum('bhqk,bkhd->bqhd', a.astype(v.dtype), v)
    return out.reshape(b, nq, W_SB)


def prompt_blocks(n, topk):
    blocks = []
    for q0 in range(0, n, Q_BLOCK):
        q1 = min(n, q0 + Q_BLOCK)
        last = q1 - 1
        if last < N_META:
            kend_a = N_META
        else:
            kend_a = N_META + ((last - N_META) // CHUNK + 1) * CHUNK
        kend_a = min(n, max(kend_a, topk))
        blocks.append((q0, q1, kend_a, q1))
    return blocks


def merge_branches(x, y_a, y_s, g_a, g_s, w_pa, w_ps, w_o):
    m = jax.nn.sigmoid(g_a) * (y_a @ w_pa) + jax.nn.sigmoid(g_s) * (y_s @ w_ps)
    return x + m @ w_o


def conv_ffn(xn, prev_rows, w_up_l, conv_w_l, conv_b_l, w_down_l):
    a, u = jnp.split(xn @ w_up_l, 2, axis=-1)
    ext = jnp.concatenate([prev_rows.astype(a.dtype), a], axis=1)
    n = a.shape[1]
    c = conv_b_l + sum(conv_w_l[i] * ext[:, i:i + n] for i in range(CONV_W))
    h = jax.nn.silu(c) * u
    return h @ w_down_l, ext[:, -(CONV_W - 1):]


def setup_inputs(seed: int = 0) -> dict:
    key = jax.random.key(seed)
    ks = jax.random.split(key, 24)
    f32 = jnp.float32
    nrm = lambda k, shape: jax.random.normal(k, shape, f32)
    return {
        "x_prompt": nrm(ks[0], (BATCH, SEQ, D_MODEL)),
        "x_sample": nrm(ks[1], (DEC_BATCH, DEC_SEQ, D_MODEL)),
        "cache_a_k": nrm(ks[2], (DEPTH, DEC_BATCH, PAST_LEN, N_KV_A, HEAD_DIM)),
        "cache_a_v": nrm(ks[3], (DEPTH, DEC_BATCH, PAST_LEN, N_KV_A, HEAD_DIM)),
        "cache_idx_k": nrm(ks[4], (DEPTH, DEC_BATCH, PAST_LEN, IDX_DIM)),
        "cache_sb_k": nrm(ks[5], (DEPTH, DEC_BATCH, PAST_LEN, N_HEADS_SB, SB_HEAD_DIM)),
        "cache_sb_v": nrm(ks[6], (DEPTH, DEC_BATCH, PAST_LEN, N_HEADS_SB, SB_HEAD_DIM)),
        "state_ffn_conv": nrm(ks[7], (DEPTH, DEC_BATCH, CONV_W - 1, D_FF)),
        "meta_tokens": nrm(ks[8], (N_META, D_MODEL)),
        "rel_bias": 0.5 * nrm(ks[9], (N_BUCKETS, N_HEADS_A)),
        "ln_mix_g": 1.0 + 0.02 * nrm(ks[10], (DEPTH, D_MODEL)),
        "w_in": nrm(ks[11], (DEPTH, D_MODEL, IN_COLS)) * D_MODEL ** -0.5,
        "q_norm_g": 1.0 + 0.02 * nrm(ks[12], (DEPTH, HEAD_DIM)),
        "k_norm_g": 1.0 + 0.02 * nrm(ks[13], (DEPTH, HEAD_DIM)),
        "w_proj_a": nrm(ks[14], (DEPTH, W_A, D_MODEL)) * W_A ** -0.5,
        "w_proj_sb": nrm(ks[15], (DEPTH, W_SB, D_MODEL)) * W_SB ** -0.5,
        "w_out": nrm(ks[16], (DEPTH, D_MODEL, D_MODEL)) * D_MODEL ** -0.5,
        "ln_ffn_g": 1.0 + 0.02 * nrm(ks[17], (DEPTH, D_MODEL)),
        "w_up": nrm(ks[18], (DEPTH, D_MODEL, 2 * D_FF)) * D_MODEL ** -0.5,
        "conv_w": 0.5 * nrm(ks[19], (DEPTH, CONV_W, D_FF)),
        "conv_b": 0.01 * nrm(ks[20], (DEPTH, D_FF)),
        "w_down": nrm(ks[21], (DEPTH, D_FF, D_MODEL)) * D_FF ** -0.5,
    }


def reference(x_prompt, x_sample, cache_a_k, cache_a_v, cache_idx_k, cache_sb_k, cache_sb_v,
              state_ffn_conv, meta_tokens, rel_bias, ln_mix_g, w_in, q_norm_g, k_norm_g,
              w_proj_a, w_proj_sb, w_out, ln_ffn_g, w_up, conv_w, conv_b, w_down):
    b_p = x_prompt.shape[0]
    meta = jnp.broadcast_to(meta_tokens.astype(x_prompt.dtype)[None], (b_p, N_META, D_MODEL))
    xp = jnp.concatenate([meta, x_prompt], axis=1)
    n_p = xp.shape[1]
    pos_p = jnp.arange(n_p, dtype=jnp.int32)
    chunk_p = jnp.where(pos_p < N_META, -1, (pos_p - N_META) // CHUNK)
    topk_p = min(TOPK_MAX, x_prompt.shape[1] // 4)
    blocks = prompt_blocks(n_p, topk_p)

    xs = x_sample
    n_s = xs.shape[1]
    past = cache_a_k.shape[2]
    kpos_s = jnp.arange(past + n_s, dtype=jnp.int32)
    kchunk_s = kpos_s // CHUNK
    qpos_s = past + jnp.arange(n_s, dtype=jnp.int32)
    qchunk_s = qpos_s // CHUNK
    topk_s = min(TOPK_MAX, (past + n_s) // 4)

    p_ak, p_av, p_ik, p_sk, p_sv, p_cv = [], [], [], [], [], []
    s_ak, s_av, s_ik, s_sk, s_sv, s_cv = [], [], [], [], [], []

    for l in range(DEPTH):
        xn = rms_norm(xp, ln_mix_g[l])
        q_a, k_a, v_a, q_i, k_i, w_i, q_s, k_s, v_s, g_a, g_s = mixer_projections(xn, w_in[l], q_norm_g[l], k_norm_g[l])
        y_a = jnp.concatenate([
            dsa_attend(q_a[:, q0:q1], q_i[:, q0:q1], w_i[:, q0:q1], pos_p[q0:q1], chunk_p[q0:q1],
                       k_a[:, :ka], v_a[:, :ka], k_i[:, :ka], pos_p[:ka], chunk_p[:ka], rel_bias, topk_p)
            for (q0, q1, ka, kb) in blocks], axis=1)
        y_s = jnp.concatenate([
            sb_attend(q_s[:, q0:q1], k_s[:, :kb], v_s[:, :kb], pos_p[q0:q1], pos_p[:kb])
            for (q0, q1, ka, kb) in blocks], axis=1)
        xp = merge_branches(xp, y_a, y_s, g_a, g_s, w_proj_a[l], w_proj_sb[l], w_out[l])
        p_ak.append(k_a); p_av.append(v_a); p_ik.append(k_i); p_sk.append(k_s); p_sv.append(v_s)
        zeros = jnp.zeros((b_p, CONV_W - 1, D_FF), xp.dtype)
        f, conv_state = conv_ffn(rms_norm(xp, ln_ffn_g[l]), zeros, w_up[l], conv_w[l], conv_b[l], w_down[l])
        xp = xp + f
        p_cv.append(conv_state)

        xn = rms_norm(xs, ln_mix_g[l])
        q_a, k_a, v_a, q_i, k_i, w_i, q_s, k_s, v_s, g_a, g_s = mixer_projections(xn, w_in[l], q_norm_g[l], k_norm_g[l])
        k_a_all = jnp.concatenate([cache_a_k[l].astype(k_a.dtype), k_a], axis=1)
        v_a_all = jnp.concatenate([cache_a_v[l].astype(v_a.dtype), v_a], axis=1)
        k_i_all = jnp.concatenate([cache_idx_k[l].astype(k_i.dtype), k_i], axis=1)
        k_s_all = jnp.concatenate([cache_sb_k[l].astype(k_s.dtype), k_s], axis=1)
        v_s_all = jnp.concatenate([cache_sb_v[l].astype(v_s.dtype), v_s], axis=1)
        y_a = dsa_attend(q_a, q_i, w_i, qpos_s, qchunk_s, k_a_all, v_a_all, k_i_all, kpos_s, kchunk_s, rel_bias, topk_s)
        y_s = sb_attend(q_s, k_s_all, v_s_all, qpos_s, kpos_s)
        xs = merge_branches(xs, y_a, y_s, g_a, g_s, w_proj_a[l], w_proj_sb[l], w_out[l])
        s_ak.append(k_a); s_av.append(v_a); s_ik.append(k_i); s_sk.append(k_s); s_sv.append(v_s)
        f, conv_state = conv_ffn(rms_norm(xs, ln_ffn_g[l]), state_ffn_conv[l], w_up[l], conv_w[l], conv_b[l], w_down[l])
        xs = xs + f
        s_cv.append(conv_state)

    y_prompt = xp[:, N_META:]
    y_sample = xs
    return (y_prompt, y_sample,
            jnp.stack(p_ak), jnp.stack(p_av), jnp.stack(p_ik), jnp.stack(p_sk), jnp.stack(p_sv), jnp.stack(p_cv),
            jnp.stack(s_ak), jnp.stack(s_av), jnp.stack(s_ik), jnp.stack(s_sk), jnp.stack(s_sv), jnp.stack(s_cv))
```

```python
import functools
import math

import numpy as np
import jax
import jax.numpy as jnp
from jax import lax
from jax.experimental import pallas as pl
from jax.experimental.pallas import tpu as pltpu

CHUNK = 64
HEAD_DIM = 64
N_HEADS_A = 8
N_IDX_HEADS = 4
IDX_DIM = 64
TOPK_MAX = 256
N_HEADS_SB = 4
SB_HEAD_DIM = 128
N_BUCKETS = 32
MAX_DISTANCE = 128
CONV_W = 3
EPS = 1e-6

W_A = N_HEADS_A * HEAD_DIM
W_IDX_Q = N_IDX_HEADS * IDX_DIM
W_SB = N_HEADS_SB * SB_HEAD_DIM

BLK = 128
SUBLANES = 8
ROW_TILE_MAX = 640
VMEM_LIMIT = 56 * 1024 * 1024
MXU_DTYPE = jnp.bfloat16
INT_MIN = -2 ** 31
SB_UNDERFLOW = -104.0

F32 = jnp.float32


def _cparams(sem):
    return pltpu.CompilerParams(dimension_semantics=sem, vmem_limit_bytes=VMEM_LIMIT)


def _row_tile(n_rows, cap=ROW_TILE_MAX):
    if n_rows <= BLK:
        return n_rows
    best = BLK
    t = BLK
    while t <= min(cap, n_rows):
        if n_rows % t == 0:
            best = t
        t += BLK
    return best


def _rms(x, g):
    return x * lax.rsqrt(jnp.mean(x * x, axis=-1, keepdims=True) + EPS) * g


def _dot(a, b):
    return jnp.dot(a, b, preferred_element_type=F32)


def _split_dot(a, b):
    hi = a.astype(MXU_DTYPE)
    lo = (a - hi.astype(F32)).astype(MXU_DTYPE)
    return _dot(hi, b) + _dot(lo, b)


def _proj_a_kernel(x_ref, g_ref, w_ref, qg_ref, kg_ref, seg_ref,
                   qa_ref, qi_ref, kvk_ref, ka_ref, va_ref, ki_ref):
    xn = _rms(x_ref[...], g_ref[...])
    y = _dot(xn.astype(MXU_DTYPE), w_ref[...])
    q = y[:, :W_A]
    ms = _split_dot(q * q, seg_ref[...])
    qa_ref[...] = (q * lax.rsqrt(ms + EPS) * qg_ref[...]).astype(qa_ref.dtype)
    qi_ref[...] = y[:, W_A:W_A + W_IDX_Q].astype(qi_ref.dtype)
    kvk = y[:, W_A + W_IDX_Q:]
    lane = lax.broadcasted_iota(jnp.int32, kvk.shape, 1)
    is_k = lane < HEAD_DIM
    msk = jnp.sum(jnp.where(is_k, kvk * kvk, 0.0), axis=-1, keepdims=True) * (1.0 / HEAD_DIM)
    kvk = jnp.where(is_k, kvk * lax.rsqrt(msk + EPS) * kg_ref[...], kvk)
    kvk_ref[...] = kvk
    ka_ref[...] = kvk[:, 0:HEAD_DIM].astype(ka_ref.dtype)
    va_ref[...] = kvk[:, HEAD_DIM:2 * HEAD_DIM].astype(va_ref.dtype)
    ki_ref[...] = kvk[:, 2 * HEAD_DIM:2 * HEAD_DIM + IDX_DIM].astype(ki_ref.dtype)


def _proj_b_kernel(x_ref, g_ref, w_ref, qs_ref, ks_ref, vs_ref, ksb_ref, vsb_ref):
    xn = _rms(x_ref[...], g_ref[...])
    y = _dot(xn.astype(MXU_DTYPE), w_ref[...])
    qs_ref[...] = y[:, :W_SB].astype(qs_ref.dtype)
    k = y[:, W_SB:2 * W_SB]
    v = y[:, 2 * W_SB:]
    ks_ref[...] = k
    vs_ref[...] = v
    ksb_ref[...] = k.astype(ksb_ref.dtype)
    vsb_ref[...] = v.astype(vsb_ref.dtype)


def _proj_c_kernel(x_ref, g_ref, w_ref, o_ref):
    xn = _rms(x_ref[...], g_ref[...])
    o_ref[...] = _dot(xn.astype(MXU_DTYPE), w_ref[...])


def _rowwise_call(kernel, x, consts, out_cols_dtypes, name):
    m, d = x.shape
    tm = _row_tile(m)
    in_specs = [pl.BlockSpec((tm, d), lambda i: (i, 0))]
    for c in consts:
        in_specs.append(pl.BlockSpec(c.shape, lambda i, nd=c.ndim: (0,) * nd))
    out_shape = [jax.ShapeDtypeStruct((m, n), dt) for n, dt in out_cols_dtypes]
    out_specs = [pl.BlockSpec((tm, n), lambda i: (i, 0)) for n, _ in out_cols_dtypes]
    return pl.pallas_call(
        kernel, grid=(m // tm,), in_specs=in_specs, out_specs=out_specs, out_shape=out_shape,
        compiler_params=_cparams(("arbitrary",)), name=name)(x, *consts)


def _t5_bucket_np(rel):
    half = N_BUCKETS // 2
    max_exact = half // 2
    n = np.abs(rel)
    large = np.full(n.shape, max_exact, dtype=np.int64)
    steps = half - max_exact
    for t in range(1, steps + 1):
        lhs = n.astype(object) ** steps
        rhs = (max_exact ** steps) * ((MAX_DISTANCE // max_exact) ** t)
        large = large + (np.array(lhs >= rhs, dtype=bool)).astype(np.int64)
    large = np.minimum(large, half - 1)
    return np.where(rel > 0, half, 0) + np.where(n < max_exact, n, large)


def _near_bucket_table():
    m = np.arange(2 * BLK)
    rows = [_t5_bucket_np(d * BLK + BLK - m) for d in (-1, 0, 1)]
    return np.stack(rows).astype(np.int32)


def _pair_transpose(x, n_pairs):
    cols = []
    for p in range(n_pairs):
        t = x[:, p * BLK:(p + 1) * BLK].T
        cols.append(t[:HEAD_DIM])
        cols.append(t[HEAD_DIM:])
    return jnp.concatenate(cols, axis=1)


def _dsa_kernel(qa_ref, qi_ref, kvk_ref, ka_ref, ki_ref, vat_ref, rbt_ref, bidx_ref, o_ref,
                key_scr, tab_scr, m_scr, l_scr, acc_scr, tie_scr,
                *, qb0, nkb_total, n_valid, chunk_off, topk):
    i = pl.program_id(1) + qb0
    nkb = jnp.minimum(i + 2, nkb_total)
    n_far = jnp.maximum(i - 1, 0)
    nh = N_HEADS_A

    qat = (_pair_transpose(qa_ref[...].astype(F32), nh // 2) * (HEAD_DIM ** -0.5)).astype(MXU_DTYPE)
    qit = _pair_transpose(qi_ref[...].astype(F32), N_IDX_HEADS // 2).astype(MXU_DTYPE)
    kw_t = kvk_ref[...][:, BLK:2 * BLK].T
    w_scale = (IDX_DIM ** -0.5) * (N_IDX_HEADS ** -0.5)
    w_rows = [kw_t[IDX_DIM + h:IDX_DIM + h + 1, :] * w_scale for h in range(N_IDX_HEADS)]

    qpos = i * BLK + lax.broadcasted_iota(jnp.int32, (1, BLK), 1)
    kend = (((qpos + (CHUNK - chunk_off)) >> 6) << 6) + chunk_off
    kend = jnp.minimum(kend, n_valid)
    krow = lax.broadcasted_iota(jnp.int32, (BLK, BLK), 0)
    kcol = lax.broadcasted_iota(jnp.int32, (BLK, BLK), 1)

    rbt = rbt_ref[...]
    far_bucket = N_BUCKETS // 2 - 1
    cfar = rbt[:, far_bucket:far_bucket + 1]
    for d in range(3):
        idx = bidx_ref[d:d + 1, :]
        tab = jnp.zeros((nh, 2 * BLK), F32)
        for b in range(N_BUCKETS):
            tab = jnp.where(idx == b, rbt[:, b:b + 1], tab)
        tab_scr[d] = tab - cfar

    def score_block(j, carry):
        off = pl.multiple_of(j * BLK, BLK)
        s = jnp.maximum(_dot(ki_ref[pl.ds(off, BLK), :], qit), 0.0)
        sc = s[:, 0:BLK] * w_rows[0]
        for h in range(1, N_IDX_HEADS):
            sc = sc + s[:, h * BLK:(h + 1) * BLK] * w_rows[h]
        bits = lax.bitcast_convert_type(sc, jnp.int32)
        key = bits ^ ((bits >> 31) & 0x7FFFFFFF)
        key = jnp.where(sc == 0.0, 0, key)
        key = jnp.where((krow + off) < kend, key, INT_MIN)
        key_scr[pl.ds(off, BLK), :] = key
        return carry

    lax.fori_loop(0, nkb, score_block, 0)

    def count(thr, strict):
        def body(j, acc):
            off = pl.multiple_of(j * BLK, BLK)
            blk = key_scr[pl.ds(off, BLK), :]
            hit = (blk > thr) if strict else (blk >= thr)
            return acc + jnp.where(hit, 1.0, 0.0)
        acc = lax.fori_loop(0, nkb, body, jnp.zeros((BLK, BLK), F32))
        return jnp.sum(acc, axis=0, keepdims=True)

    kf = float(topk)
    thr0 = jnp.where(count(jnp.zeros((1, BLK), jnp.int32), False) >= kf, 0, INT_MIN).astype(jnp.int32)

    def bit_step(t, thr):
        cand = thr | (jnp.int32(1) << (30 - t))
        return jnp.where(count(cand, False) >= kf, cand, thr)

    thr = lax.fori_loop(0, 31, bit_step, thr0)
    n_tie_keep = jnp.where(thr == INT_MIN, 0.0, kf - count(thr, True))

    m_scr[...] = jnp.full(m_scr.shape, -jnp.inf, F32)
    l_scr[...] = jnp.zeros(l_scr.shape, F32)
    acc_scr[...] = jnp.zeros(acc_scr.shape, F32)
    tie_scr[...] = jnp.zeros(tie_scr.shape, F32)
    lstrict = jnp.where(kcol < krow, 1.0, 0.0).astype(MXU_DTYPE)

    def attend_block(j, near):
        off = pl.multiple_of(j * BLK, BLK)
        key = key_scr[pl.ds(off, BLK), :]
        eq = key == thr
        eqf = jnp.where(eq, 1.0, 0.0)
        ties_before = _dot(lstrict, eqf.astype(MXU_DTYPE)) + tie_scr[0:1, :]
        sel = (key > thr) | (eq & (ties_before < n_tie_keep))
        tie_scr[0:1, :] = tie_scr[0:1, :] + jnp.sum(eqf, axis=0, keepdims=True)

        lg = _dot(ka_ref[pl.ds(off, BLK), :], qat)
        if near:
            tab = tab_scr[j - i + 1]
        m_old = m_scr[0:1, :]
        lgm = []
        for h in range(nh):
            x = lg[:, h * BLK:(h + 1) * BLK]
            if near:
                trow = jnp.broadcast_to(tab[h:h + 1, :], (BLK, 2 * BLK))
                x = x + pltpu.roll(trow, 0, 1, stride=1, stride_axis=0)[:, BLK:]
            lgm.append(jnp.where(sel, x, -jnp.inf))
        lgm = jnp.concatenate(lgm, axis=1)
        m_new = jnp.maximum(m_old, jnp.max(lgm, axis=0, keepdims=True))
        m_safe = jnp.where(m_new == -jnp.inf, 0.0, m_new)
        alpha = jnp.exp(m_old - m_safe)
        p = jnp.exp(lgm - m_safe)
        l_scr[0:1, :] = l_scr[0:1, :] * alpha + jnp.sum(p, axis=0, keepdims=True)
        acc_scr[...] = acc_scr[...] * alpha + _dot(vat_ref[j], p.astype(MXU_DTYPE))
        m_scr[0:1, :] = m_new

    def far_body(j, carry):
        attend_block(j, False)
        return carry

    def near_body(j, carry):
        attend_block(j, True)
        return carry

    lax.fori_loop(0, n_far, far_body, 0)
    lax.fori_loop(n_far, nkb, near_body, 0)

    out_t = acc_scr[...] / l_scr[0:1, :]
    for p in range(nh // 2):
        pair = jnp.concatenate([out_t[:, (2 * p) * BLK:(2 * p + 1) * BLK],
                                out_t[:, (2 * p + 1) * BLK:(2 * p + 2) * BLK]], axis=0)
        o_ref[:, p * BLK:(p + 1) * BLK] = pair.T.astype(o_ref.dtype)


def _dsa_call(qa, qi, kvk, ka, ki, vat, rbt, bidx, *, qb0, n_valid, chunk_off, topk):
    b, nq, _ = qa.shape
    nk = ka.shape[1]
    nkb_total = nk // BLK
    nh = N_HEADS_A
    kernel = functools.partial(_dsa_kernel, qb0=qb0, nkb_total=nkb_total, n_valid=n_valid,
                               chunk_off=chunk_off, topk=topk)
    qspec = lambda w: pl.BlockSpec((None, BLK, w), lambda bi, qi_: (bi, qi_, 0))
    full = lambda a: pl.BlockSpec((None,) + a.shape[1:], lambda bi, qi_, nd=a.ndim: (bi,) + (0,) * (nd - 1))
    const = lambda a: pl.BlockSpec(a.shape, lambda bi, qi_, nd=a.ndim: (0,) * nd)
    return pl.pallas_call(
        kernel, grid=(b, nq // BLK),
        in_specs=[qspec(W_A), qspec(W_IDX_Q), qspec(2 * BLK), full(ka), full(ki), full(vat), const(rbt), const(bidx)],
        out_specs=qspec(W_A),
        out_shape=jax.ShapeDtypeStruct((b, nq, W_A), MXU_DTYPE),
        scratch_shapes=[
            pltpu.VMEM((nk, BLK), jnp.int32),
            pltpu.VMEM((3, nh, 2 * BLK), F32),
            pltpu.VMEM((SUBLANES, nh * BLK), F32),
            pltpu.VMEM((SUBLANES, nh * BLK), F32),
            pltpu.VMEM((HEAD_DIM, nh * BLK), F32),
            pltpu.VMEM((SUBLANES, BLK), F32),
        ],
        compiler_params=_cparams(("arbitrary", "arbitrary")), name="dsa")(qa, qi, kvk, ka, ki, vat, rbt, bidx)


def _sb_kernel(q_ref, k_ref, v_ref, o_ref, *, qb0):
    i = pl.program_id(2) + qb0
    q = q_ref[...]
    row = lax.broadcasted_iota(jnp.int32, (BLK, BLK), 0)
    col = lax.broadcasted_iota(jnp.int32, (BLK, BLK), 1)
    tri = jnp.where(row > col, 1.0, 0.0).astype(MXU_DTYPE)
    before = col < row
    scale = SB_HEAD_DIM ** -0.5

    def block(j, later_blocks, acc, diag):
        off = pl.multiple_of(j * BLK, BLK)
        kb = k_ref[pl.ds(off, BLK), :]
        vb = v_ref[pl.ds(off, BLK), :]
        z = lax.dot_general(q, kb, (((1,), (1,)), ((), ())), preferred_element_type=F32) * scale
        log_keep = -(jnp.maximum(z, 0.0) + jnp.log1p(jnp.exp(-jnp.abs(z))))
        if diag:
            log_keep = jnp.where(before, log_keep, 0.0)
        later = _split_dot(log_keep, tri) + later_blocks
        a = jnp.exp(log_keep + z + later)
        if diag:
            a = jnp.where(before, a, 0.0)
        acc = acc + _dot(a.astype(MXU_DTYPE), vb)
        later_blocks = later_blocks + jnp.sum(log_keep, axis=1, keepdims=True)
        return later_blocks, acc

    later_blocks, acc = block(i, jnp.zeros((BLK, 1), F32), jnp.zeros((BLK, SB_HEAD_DIM), F32), True)

    def cond(c):
        j, worst, _, _ = c
        return jnp.logical_and(j >= 0, worst > SB_UNDERFLOW)

    def body(c):
        j, _, lb, ac = c
        lb, ac = block(j, lb, ac, False)
        return j - 1, jnp.max(lb), lb, ac

    _, _, _, acc = lax.while_loop(cond, body, (i - 1, jnp.max(later_blocks), later_blocks, acc))
    o_ref[...] = acc.astype(o_ref.dtype)


def _sb_call(q, k, v, *, qb0):
    b, nq, _ = q.shape
    nk = k.shape[1]
    kernel = functools.partial(_sb_kernel, qb0=qb0)
    kv_spec = pl.BlockSpec((None, nk, SB_HEAD_DIM), lambda bi, h, qi_: (bi, 0, h))
    q_spec = pl.BlockSpec((None, BLK, SB_HEAD_DIM), lambda bi, h, qi_: (bi, qi_, h))
    return pl.pallas_call(
        kernel, grid=(b, N_HEADS_SB, nq // BLK),
        in_specs=[q_spec, kv_spec, kv_spec], out_specs=q_spec,
        out_shape=jax.ShapeDtypeStruct((b, nq, W_SB), MXU_DTYPE),
        compiler_params=_cparams(("arbitrary", "arbitrary", "arbitrary")), name="sb")(q, k, v)


def _merge_kernel(x_ref, ya_ref, ys_ref, g_ref, wpa_ref, wps_ref, wo_ref, o_ref):
    d = x_ref.shape[-1]
    g = g_ref[...]
    m = (jax.nn.sigmoid(g[:, :d]) * _dot(ya_ref[...], wpa_ref[...])
         + jax.nn.sigmoid(g[:, d:]) * _dot(ys_ref[...], wps_ref[...]))
    o_ref[...] = x_ref[...] + _dot(m.astype(MXU_DTYPE), wo_ref[...])


def _merge_call(x, ya, ys, g, wpa, wps, wo):
    m, d = x.shape
    tm = _row_tile(m)
    row = lambda w: pl.BlockSpec((tm, w), lambda i: (i, 0))
    const = lambda a: pl.BlockSpec(a.shape, lambda i, nd=a.ndim: (0,) * nd)
    return pl.pallas_call(
        _merge_kernel, grid=(m // tm,),
        in_specs=[row(d), row(ya.shape[1]), row(ys.shape[1]), row(g.shape[1]), const(wpa), const(wps), const(wo)],
        out_specs=row(d), out_shape=jax.ShapeDtypeStruct((m, d), F32),
        compiler_params=_cparams(("arbitrary",)), name="merge")(x, ya, ys, g, wpa, wps, wo)


def _ffn_kernel(x_ref, g_ref, st_ref, wa_ref, wu_ref, cw_ref, cb_ref, wd_ref, o_ref,
                h_scr, carry_scr, act_scr):
    t = pl.program_id(1)
    c = pl.program_id(2)
    tm = x_ref.shape[0]

    @pl.when(c == 0)
    def _():
        x = x_ref[...]
        h_scr[...] = _rms(x, g_ref[...]).astype(h_scr.dtype)
        o_ref[...] = x

    @pl.when(t == 0)
    def _():
        carry_scr[c] = st_ref[...]

    h = h_scr[...]
    a = _dot(h, wa_ref[...])
    u = _dot(h, wu_ref[...])
    w0, w1, w2 = cw_ref[0:1, :], cw_ref[1:2, :], cw_ref[2:3, :]
    bias = cb_ref[...]

    def gate(a2, a1, a0, uu):
        cv = bias + (w0 * a2 + w1 * a1 + w2 * a0)
        return cv * jax.nn.sigmoid(cv) * uu

    act_scr[...] = gate(pltpu.roll(a, 2, 0), pltpu.roll(a, 1, 0), a, u)
    prev = carry_scr[c]
    p2 = prev[SUBLANES - 2:SUBLANES - 1, :]
    p1 = prev[SUBLANES - 1:SUBLANES, :]
    top = a[0:SUBLANES]
    rid = lax.broadcasted_iota(jnp.int32, top.shape, 0)
    a1 = jnp.where(rid == 0, p1, pltpu.roll(top, 1, 0))
    a2 = jnp.where(rid == 0, p2, jnp.where(rid == 1, p1, pltpu.roll(top, 2, 0)))
    act_scr[0:SUBLANES, :] = gate(a2, a1, top, u[0:SUBLANES])
    carry_scr[c] = a[tm - SUBLANES:tm]
    o_ref[...] += _dot(act_scr[...].astype(MXU_DTYPE), wd_ref[...])


def _ffn_call(x, g, state, wa, wu, cw, cb, wd, n_chunks):
    b, n, d = x.shape
    dff = wa.shape[1]
    cwid = dff // n_chunks
    tm = _row_tile(n)
    return pl.pallas_call(
        _ffn_kernel, grid=(b, n // tm, n_chunks),
        in_specs=[
            pl.BlockSpec((None, tm, d), lambda bi, t, c: (bi, t, 0)),
            pl.BlockSpec((1, d), lambda bi, t, c: (0, 0)),
            pl.BlockSpec((None, SUBLANES, cwid), lambda bi, t, c: (bi, 0, c)),
            pl.BlockSpec((d, cwid), lambda bi, t, c: (0, c)),
            pl.BlockSpec((d, cwid), lambda bi, t, c: (0, c)),
            pl.BlockSpec((CONV_W, cwid), lambda bi, t, c: (0, c)),
            pl.BlockSpec((1, cwid), lambda bi, t, c: (0, c)),
            pl.BlockSpec((cwid, d), lambda bi, t, c: (c, 0)),
        ],
        out_specs=pl.BlockSpec((None, tm, d), lambda bi, t, c: (bi, t, 0)),
        out_shape=jax.ShapeDtypeStruct((b, n, d), F32),
        scratch_shapes=[
            pltpu.VMEM((tm, d), MXU_DTYPE),
            pltpu.VMEM((n_chunks, SUBLANES, cwid), F32),
            pltpu.VMEM((tm, cwid), F32),
        ],
        compiler_params=_cparams(("arbitrary", "arbitrary", "arbitrary")), name="ffn")(
            x, g, state, wa, wu, cw, cb, wd)


def _ffn_chunks(dff):
    return 2 if dff % (2 * BLK) == 0 else 1


def _layer(x, kv_prefix, conv_state, lw, *, n_new, qb0, n_valid, chunk_off, topk, rbt, bidx):
    b, nq, d = x.shape
    xf = x.reshape(b * nq, d)
    g_mix = lw["ln_mix_g"]
    qa, qi, kvk, ka, va, ki = _rowwise_call(
        _proj_a_kernel, xf, [g_mix, lw["w_a"], lw["qg"], lw["kg"], lw["seg"]],
        [(W_A, MXU_DTYPE), (W_IDX_Q, MXU_DTYPE), (2 * BLK, F32), (HEAD_DIM, MXU_DTYPE), (HEAD_DIM, MXU_DTYPE),
         (IDX_DIM, MXU_DTYPE)], "proj_a")
    qs, ks, vs, ksb, vsb = _rowwise_call(
        _proj_b_kernel, xf, [g_mix, lw["w_b"]],
        [(W_SB, MXU_DTYPE), (W_SB, F32), (W_SB, F32), (W_SB, MXU_DTYPE), (W_SB, MXU_DTYPE)], "proj_b")
    (gates,) = _rowwise_call(_proj_c_kernel, xf, [g_mix, lw["w_c"]], [(2 * d, F32)], "proj_c")

    r3 = lambda a: a.reshape(b, nq, a.shape[-1])
    qa, qi, kvk, ka, va, ki, qs, ks, vs, ksb, vsb = map(r3, (qa, qi, kvk, ka, va, ki, qs, ks, vs, ksb, vsb))

    if kv_prefix is None:
        ka_all, va_all, ki_all, ks_all, vs_all = ka, va, ki, ksb, vsb
    else:
        def join(prefix, new):
            cat = jnp.concatenate([prefix, new[:, :n_new]], axis=1)
            pad = (-cat.shape[1]) % BLK
            return jnp.pad(cat, ((0, 0), (0, pad), (0, 0)))
        ka_all, va_all, ki_all, ks_all, vs_all = (join(p, n_) for p, n_ in zip(kv_prefix, (ka, va, ki, ksb, vsb)))
    nk = ka_all.shape[1]
    vat = va_all.reshape(b, nk // BLK, BLK, HEAD_DIM).transpose(0, 1, 3, 2)

    ya = _dsa_call(qa, qi, kvk, ka_all, ki_all, vat, rbt, bidx,
                   qb0=qb0, n_valid=n_valid, chunk_off=chunk_off, topk=topk)
    ys = _sb_call(qs, ks_all, vs_all, qb0=qb0)
    x_mid = _merge_call(xf, ya.reshape(b * nq, W_A), ys.reshape(b * nq, W_SB), gates,
                        lw["w_pa"], lw["w_ps"], lw["w_o"]).reshape(b, nq, d)

    last = x_mid[:, n_new - (CONV_W - 1):n_new].reshape(b * (CONV_W - 1), d)
    (conv_rows,) = _rowwise_call(_proj_c_kernel, last, [lw["ln_ffn_g"], lw["w_up_a"]],
                                 [(lw["w_up_a"].shape[1], F32)], "conv_state")
    x_out = _ffn_call(x_mid, lw["ln_ffn_g"], conv_state, lw["w_up_a"], lw["w_up_u"], lw["conv_w"], lw["conv_b"],
                      lw["w_down"], _ffn_chunks(lw["w_up_a"].shape[1]))
    new_rows = dict(a_k=kvk[:, :n_new, 0:HEAD_DIM], a_v=kvk[:, :n_new, HEAD_DIM:2 * HEAD_DIM],
                    idx_k=kvk[:, :n_new, 2 * HEAD_DIM:2 * HEAD_DIM + IDX_DIM],
                    sb_k=ks[:, :n_new], sb_v=vs[:, :n_new],
                    conv=conv_rows.reshape(b, CONV_W - 1, -1))
    return x_out, new_rows


def kernel(x_prompt, x_sample, cache_a_k, cache_a_v, cache_idx_k, cache_sb_k, cache_sb_v, state_ffn_conv, meta_tokens, rel_bias, ln_mix_g, w_in, q_norm_g, k_norm_g, w_proj_a, w_proj_sb, w_out, ln_ffn_g, w_up, conv_w, conv_b, w_down):
    depth, d_model, _ = w_in.shape
    b_p, seq, _ = x_prompt.shape
    n_meta = meta_tokens.shape[0]
    b_s, n_s, _ = x_sample.shape
    past = cache_a_k.shape[2]
    d_ff = w_down.shape[1]
    n_p = n_meta + seq
    topk_p = min(TOPK_MAX, seq // 4)
    topk_s = min(TOPK_MAX, (past + n_s) // 4)
    assert n_meta <= CHUNK and CONV_W - 1 <= min(n_s, SUBLANES)

    sizes = (W_A, HEAD_DIM, HEAD_DIM, W_IDX_Q, IDX_DIM, N_IDX_HEADS, W_SB, W_SB, W_SB, d_model, d_model)
    offs = np.concatenate([[0], np.cumsum(sizes)])
    col = lambda k: w_in[:, :, offs[k]:offs[k + 1]]
    w_pad = jnp.zeros((depth, d_model, 2 * BLK - 3 * HEAD_DIM - N_IDX_HEADS), w_in.dtype)
    w_a = jnp.concatenate([col(0), col(3), col(1), col(2), col(4), col(5), w_pad], axis=-1).astype(MXU_DTYPE)
    w_b = w_in[:, :, offs[6]:offs[9]].astype(MXU_DTYPE)
    w_c = w_in[:, :, offs[9]:offs[11]].astype(MXU_DTYPE)
    seg_np = np.kron(np.eye(N_HEADS_A), np.full((HEAD_DIM, HEAD_DIM), 1.0 / HEAD_DIM))
    seg = jnp.asarray(seg_np, MXU_DTYPE)
    kg_pad = jnp.concatenate([k_norm_g, jnp.ones((depth, 2 * BLK - HEAD_DIM), k_norm_g.dtype)], axis=-1)
    rbt = rel_bias.T.astype(F32)
    bidx = jnp.asarray(_near_bucket_table())

    layers = []
    for l in range(depth):
        layers.append(dict(
            ln_mix_g=ln_mix_g[l][None], w_a=w_a[l], w_b=w_b[l], w_c=w_c[l],
            qg=jnp.tile(q_norm_g[l], N_HEADS_A)[None], kg=kg_pad[l][None], seg=seg,
            w_pa=w_proj_a[l].astype(MXU_DTYPE), w_ps=w_proj_sb[l].astype(MXU_DTYPE), w_o=w_out[l].astype(MXU_DTYPE),
            ln_ffn_g=ln_ffn_g[l][None], w_up_a=w_up[l][:, :d_ff].astype(MXU_DTYPE),
            w_up_u=w_up[l][:, d_ff:].astype(MXU_DTYPE), conv_w=conv_w[l], conv_b=conv_b[l][None],
            w_down=w_down[l].astype(MXU_DTYPE)))

    np_pad = -(-n_p // BLK) * BLK
    meta = jnp.broadcast_to(meta_tokens.astype(x_prompt.dtype)[None], (b_p, n_meta, d_model))
    xp = jnp.concatenate([meta, x_prompt, jnp.zeros((b_p, np_pad - n_p, d_model), x_prompt.dtype)], axis=1)
    zero_state = jnp.zeros((b_p, SUBLANES, d_ff), F32)

    ns_pad = -(-n_s // BLK) * BLK
    assert past % BLK == 0 and ns_pad == BLK
    xs = jnp.pad(x_sample, ((0, 0), (0, ns_pad - n_s), (0, 0)))

    outs_p, outs_s = [], []
    for l in range(depth):
        lw = layers[l]
        xp, rows_p = _layer(xp, None, zero_state, lw, n_new=n_p, qb0=0, n_valid=n_p, chunk_off=n_meta,
                            topk=topk_p, rbt=rbt, bidx=bidx)
        outs_p.append(rows_p)

        prefix = (cache_a_k[l].reshape(b_s, past, HEAD_DIM).astype(MXU_DTYPE),
                  cache_a_v[l].reshape(b_s, past, HEAD_DIM).astype(MXU_DTYPE),
                  cache_idx_k[l].astype(MXU_DTYPE),
                  cache_sb_k[l].reshape(b_s, past, W_SB).astype(MXU_DTYPE),
                  cache_sb_v[l].reshape(b_s, past, W_SB).astype(MXU_DTYPE))
        st = jnp.pad(state_ffn_conv[l].astype(F32), ((0, 0), (SUBLANES - (CONV_W - 1), 0), (0, 0)))
        xs, rows_s = _layer(xs, prefix, st, lw, n_new=n_s, qb0=past // BLK, n_valid=past + n_s, chunk_off=0,
                            topk=topk_s, rbt=rbt, bidx=bidx)
        outs_s.append(rows_s)

    def stack(outs, name, shape_tail):
        a = jnp.stack([o[name] for o in outs])
        return a.reshape(a.shape[:3] + shape_tail)

    def group(outs):
        return (stack(outs, "a_k", (1, HEAD_DIM)), stack(outs, "a_v", (1, HEAD_DIM)), stack(outs, "idx_k", (IDX_DIM,)),
                stack(outs, "sb_k", (N_HEADS_SB, SB_HEAD_DIM)), stack(outs, "sb_v", (N_HEADS_SB, SB_HEAD_DIM)),
                stack(outs, "conv", (d_ff,)))

    y_prompt = xp[:, n_meta:n_p]
    y_sample = xs[:, :n_s]
    return (y_prompt, y_sample) + group(outs_p) + group(outs_s)
```

```python
import functools
import math

import numpy as np
import jax
import jax.numpy as jnp
from jax import lax
from jax.experimental import pallas as pl
from jax.experimental.pallas import tpu as pltpu

CHUNK = 64
HEAD_DIM = 64
N_HEADS_A = 8
N_IDX_HEADS = 4
IDX_DIM = 64
TOPK_MAX = 256
N_HEADS_SB = 4
SB_HEAD_DIM = 128
N_BUCKETS = 32
MAX_DISTANCE = 128
CONV_W = 3
EPS = 1e-6

W_A = N_HEADS_A * HEAD_DIM
W_IDX_Q = N_IDX_HEADS * IDX_DIM
W_SB = N_HEADS_SB * SB_HEAD_DIM

BLK = 128
DSA_UNROLL = 4
DENOM_ROWS = 16
LOG2E = math.log2(math.e)
SUBLANES = 8
ROW_TILE_MAX = 640
VMEM_LIMIT = 56 * 1024 * 1024
MXU_DTYPE = jnp.bfloat16
INT_MIN = -2 ** 31
SB_UNDERFLOW = -104.0

F32 = jnp.float32


def _cparams(sem):
    return pltpu.CompilerParams(dimension_semantics=sem, vmem_limit_bytes=VMEM_LIMIT)


def _row_tile(n_rows, cap=ROW_TILE_MAX):
    if n_rows <= BLK:
        return n_rows
    best = BLK
    t = BLK
    while t <= min(cap, n_rows):
        if n_rows % t == 0:
            best = t
        t += BLK
    return best


def _rms(x, g):
    return x * lax.rsqrt(jnp.mean(x * x, axis=-1, keepdims=True) + EPS) * g


def _dot(a, b):
    return jnp.dot(a, b, preferred_element_type=F32)


def _split_dot(a, b):
    hi = a.astype(MXU_DTYPE)
    lo = (a - hi.astype(F32)).astype(MXU_DTYPE)
    return _dot(hi, b) + _dot(lo, b)


def _proj_a_kernel(x_ref, g_ref, w_ref, qg_ref, kg_ref, seg_ref,
                   qa_ref, qi_ref, kvk_ref, ka_ref, va_ref, ki_ref):
    xn = _rms(x_ref[...], g_ref[...])
    y = _dot(xn.astype(MXU_DTYPE), w_ref[...])
    q = y[:, :W_A]
    ms = _split_dot(q * q, seg_ref[...])
    qa_ref[...] = (q * lax.rsqrt(ms + EPS) * qg_ref[...]).astype(qa_ref.dtype)
    qi_ref[...] = y[:, W_A:W_A + W_IDX_Q].astype(qi_ref.dtype)
    kvk = y[:, W_A + W_IDX_Q:]
    lane = lax.broadcasted_iota(jnp.int32, kvk.shape, 1)
    is_k = lane < HEAD_DIM
    msk = jnp.sum(jnp.where(is_k, kvk * kvk, 0.0), axis=-1, keepdims=True) * (1.0 / HEAD_DIM)
    kvk = jnp.where(is_k, kvk * lax.rsqrt(msk + EPS) * kg_ref[...], kvk)
    kvk_ref[...] = kvk
    ka_ref[...] = kvk[:, 0:HEAD_DIM].astype(ka_ref.dtype)
    va_ref[...] = kvk[:, HEAD_DIM:2 * HEAD_DIM].astype(va_ref.dtype)
    ki_ref[...] = kvk[:, 2 * HEAD_DIM:2 * HEAD_DIM + IDX_DIM].astype(ki_ref.dtype)


def _proj_b_kernel(x_ref, g_ref, w_ref, qs_ref, ks_ref, vs_ref, ksb_ref, vsb_ref):
    xn = _rms(x_ref[...], g_ref[...])
    y = _dot(xn.astype(MXU_DTYPE), w_ref[...])
    qs_ref[...] = y[:, :W_SB].astype(qs_ref.dtype)
    k = y[:, W_SB:2 * W_SB]
    v = y[:, 2 * W_SB:]
    ks_ref[...] = k
    vs_ref[...] = v
    ksb_ref[...] = k.astype(ksb_ref.dtype)
    vsb_ref[...] = v.astype(vsb_ref.dtype)


def _proj_c_kernel(x_ref, g_ref, w_ref, o_ref):
    xn = _rms(x_ref[...], g_ref[...])
    o_ref[...] = _dot(xn.astype(MXU_DTYPE), w_ref[...])


def _rowwise_call(kernel, x, consts, out_cols_dtypes, name):
    m, d = x.shape
    tm = _row_tile(m)
    in_specs = [pl.BlockSpec((tm, d), lambda i: (i, 0))]
    for c in consts:
        in_specs.append(pl.BlockSpec(c.shape, lambda i, nd=c.ndim: (0,) * nd))
    out_shape = [jax.ShapeDtypeStruct((m, n), dt) for n, dt in out_cols_dtypes]
    out_specs = [pl.BlockSpec((tm, n), lambda i: (i, 0)) for n, _ in out_cols_dtypes]
    return pl.pallas_call(
        kernel, grid=(m // tm,), in_specs=in_specs, out_specs=out_specs, out_shape=out_shape,
        compiler_params=_cparams(("arbitrary",)), name=name)(x, *consts)


def _t5_bucket_np(rel):
    half = N_BUCKETS // 2
    max_exact = half // 2
    n = np.abs(rel)
    large = np.full(n.shape, max_exact, dtype=np.int64)
    steps = half - max_exact
    for t in range(1, steps + 1):
        lhs = n.astype(object) ** steps
        rhs = (max_exact ** steps) * ((MAX_DISTANCE // max_exact) ** t)
        large = large + (np.array(lhs >= rhs, dtype=bool)).astype(np.int64)
    large = np.minimum(large, half - 1)
    return np.where(rel > 0, half, 0) + np.where(n < max_exact, n, large)


def _near_bucket_table():
    m = np.arange(2 * BLK)
    rows = [_t5_bucket_np(d * BLK + BLK - m) for d in (-1, 0, 1)]
    return np.stack(rows).astype(np.int32)


def _pair_transpose(x, n_pairs):
    cols = []
    for p in range(n_pairs):
        t = x[:, p * BLK:(p + 1) * BLK].T
        cols.append(t[:HEAD_DIM])
        cols.append(t[HEAD_DIM:])
    return jnp.concatenate(cols, axis=1)


def _dsa_kernel(qa_ref, qi_ref, kvk_ref, ka_ref, ki_ref, vat_ref, rbt_ref, bidx_ref, o_ref,
                key_scr, bias_scr, m_scr, acc_scr, tie_scr, lg0_scr, lg1_scr,
                *, qb0, nsb_total, n_valid, chunk_off, topk):
    i = pl.program_id(1) + qb0
    nsb = jnp.minimum((i + 2 + DSA_UNROLL - 1) // DSA_UNROLL, nsb_total)
    nh = N_HEADS_A
    sbk = DSA_UNROLL * BLK

    qat = (_pair_transpose(qa_ref[...].astype(F32), nh // 2) * (HEAD_DIM ** -0.5 * LOG2E)).astype(MXU_DTYPE)
    qit = _pair_transpose(qi_ref[...].astype(F32), N_IDX_HEADS // 2).astype(MXU_DTYPE)
    kw_t = kvk_ref[...][:, BLK:2 * BLK].T
    w_scale = (IDX_DIM ** -0.5) * (N_IDX_HEADS ** -0.5)
    w_rows = [kw_t[IDX_DIM + h:IDX_DIM + h + 1, :] * w_scale for h in range(N_IDX_HEADS)]

    qpos = i * BLK + lax.broadcasted_iota(jnp.int32, (1, BLK), 1)
    kend = (((qpos + (CHUNK - chunk_off)) >> 6) << 6) + chunk_off
    kend = jnp.minimum(kend, n_valid)
    krow = lax.broadcasted_iota(jnp.int32, (sbk, BLK), 0)

    @pl.when(jnp.logical_and(pl.program_id(0) == 0, pl.program_id(1) == 0))
    def _():
        rbt = rbt_ref[...]
        far_bucket = N_BUCKETS // 2 - 1
        cfar = rbt[:, far_bucket:far_bucket + 1]
        bias_scr[0] = jnp.zeros((BLK, nh * BLK), F32)
        for d in range(3):
            idx = bidx_ref[d:d + 1, :]
            tab = jnp.zeros((nh, 2 * BLK), F32)
            for b in range(N_BUCKETS):
                tab = jnp.where(idx == b, rbt[:, b:b + 1], tab)
            tab = (tab - cfar) * LOG2E
            for h in range(nh):
                trow = jnp.broadcast_to(tab[h:h + 1, :], (BLK, 2 * BLK))
                bias_scr[d + 1, :, h * BLK:(h + 1) * BLK] = pltpu.roll(trow, 0, 1, stride=1, stride_axis=0)[:, BLK:]

    def score_block(s_, carry):
        off = pl.multiple_of(s_ * sbk, sbk)
        s = jnp.maximum(_dot(ki_ref[pl.ds(off, sbk), :], qit), 0.0)
        sc = s[:, 0:BLK] * w_rows[0]
        for h in range(1, N_IDX_HEADS):
            sc = sc + s[:, h * BLK:(h + 1) * BLK] * w_rows[h]
        bits = lax.bitcast_convert_type(sc, jnp.int32)
        key = bits ^ ((bits >> 31) & 0x7FFFFFFF)
        key = jnp.where(sc == 0.0, 0, key)
        key = jnp.where((krow + off) < kend, key, INT_MIN)
        key_scr[pl.ds(off, sbk), :] = key
        return carry

    lax.fori_loop(0, nsb, score_block, 0)

    n_acc = 8 * SUBLANES

    def count(thr, strict):
        def body(s_, acc):
            off = pl.multiple_of(s_ * sbk, sbk)
            blk = key_scr[pl.ds(off, sbk), :]
            hit = (blk > thr) if strict else (blk >= thr)
            return acc + jnp.sum(jnp.where(hit, 1.0, 0.0).reshape(sbk // n_acc, n_acc, BLK), axis=0)
        acc = lax.fori_loop(0, nsb, body, jnp.zeros((n_acc, BLK), F32))
        return jnp.sum(acc, axis=0, keepdims=True)

    kf = float(topk)
    zero = jnp.zeros((1, BLK), jnp.int32)
    c_nonneg = count(zero, False)
    c_pos = count(zero, True)
    few = kend.astype(F32) <= kf
    thr0 = jnp.where(few, INT_MIN, jnp.where(c_nonneg == kf, -1, jnp.where(c_nonneg > kf, 0, INT_MIN)))
    thr0 = thr0.astype(jnp.int32)
    frozen0 = few | (c_nonneg == kf) | ((c_nonneg > kf) & (c_pos <= kf))
    open0 = jnp.where(frozen0, 0.0, 1.0)

    n_bits = 31
    group = 4

    def bit_cond(c):
        g, _, _, n_open = c
        return jnp.logical_and(g * group < n_bits, n_open > 0.0)

    def bit_steps(c):
        g, thr, open_, _ = c
        for u in range(group):
            t = g * group + u
            bit = jnp.where(t < n_bits, jnp.int32(1) << jnp.maximum(n_bits - 1 - t, 0), 0)
            cand = thr | bit
            cnt = count(cand, False)
            upd = jnp.where(cnt == kf, cand - 1, jnp.where(cnt > kf, cand, thr))
            thr = jnp.where(open_ > 0.0, upd, thr)
            open_ = jnp.where(cnt == kf, 0.0, open_)
        return g + 1, thr, open_, jnp.sum(open_)

    _, thr, _, _ = lax.while_loop(bit_cond, bit_steps, (jnp.int32(0), thr0, open0, jnp.sum(open0)))
    n_tie_keep = jnp.where(thr == INT_MIN, 0.0, kf - count(thr, True))

    m_scr[...] = jnp.full(m_scr.shape, -jnp.inf, F32)
    acc_scr[...] = jnp.zeros(acc_scr.shape, F32)
    tie_scr[...] = jnp.zeros(tie_scr.shape, F32)
    lrow = lax.broadcasted_iota(jnp.int32, (sbk, sbk), 0)
    lcol = lax.broadcasted_iota(jnp.int32, (sbk, sbk), 1)
    lstrict = jnp.where(lcol < lrow, 1.0, 0.0).astype(MXU_DTYPE)

    def stage_a(s_next, lg_ref):
        s_ = jnp.minimum(s_next, nsb - 1)
        off = pl.multiple_of(s_ * sbk, sbk)
        key = key_scr[pl.ds(off, sbk), :]
        eq = key == thr
        eqf = jnp.where(eq, 1.0, 0.0)
        ties_before = _dot(lstrict, eqf.astype(MXU_DTYPE)) + tie_scr[0:1, :]
        sel = (key > thr) | (eq & (ties_before < n_tie_keep))
        tie_scr[0:1, :] = tie_scr[0:1, :] + jnp.sum(eqf, axis=0, keepdims=True)
        selb = jnp.where(sel, jnp.where(s_next < nsb, 0.0, -jnp.inf), -jnp.inf)
        lg = _dot(ka_ref[pl.ds(off, sbk), :], qat)
        for u in range(DSA_UNROLL):
            d = jnp.clip(s_ * DSA_UNROLL + u - i, -2, 1) + 2
            rows = slice(u * BLK, (u + 1) * BLK)
            bias = bias_scr[d]
            for h in range(nh):
                cols = slice(h * BLK, (h + 1) * BLK)
                lg_ref[rows, cols] = lg[rows, cols] + bias[:, cols] + selb[rows, :]

    def stage_b(s_, lg_ref):
        m_old = m_scr[0:1, :]
        lgm = lg_ref[...]
        m_new = jnp.maximum(m_old, jnp.max(lgm, axis=0, keepdims=True))
        m_safe = jnp.where(m_new == -jnp.inf, 0.0, m_new)
        alpha = jnp.exp2(m_old - m_safe)
        p = jnp.exp2(lgm - m_safe)
        vat = vat_ref[jnp.minimum(s_, nsb_total - 1)]
        acc_scr[...] = acc_scr[...] * alpha + _dot(vat, p.astype(MXU_DTYPE))
        m_scr[0:1, :] = m_new

    def pair_body(pi, carry):
        s_ = 2 * pi
        stage_a(s_ + 1, lg1_scr)
        stage_b(s_, lg0_scr)
        stage_a(s_ + 2, lg0_scr)
        stage_b(s_ + 1, lg1_scr)
        return carry

    stage_a(0, lg0_scr)
    lax.fori_loop(0, (nsb + 1) // 2, pair_body, 0)

    out_t = acc_scr[0:HEAD_DIM, :] / acc_scr[HEAD_DIM:HEAD_DIM + 1, :]
    for p in range(nh // 2):
        pair = jnp.concatenate([out_t[:, (2 * p) * BLK:(2 * p + 1) * BLK],
                                out_t[:, (2 * p + 1) * BLK:(2 * p + 2) * BLK]], axis=0)
        o_ref[:, p * BLK:(p + 1) * BLK] = pair.T.astype(o_ref.dtype)


def _dsa_call(qa, qi, kvk, ka, ki, va, rbt, bidx, *, qb0, n_valid, chunk_off, topk):
    b, nq, _ = qa.shape
    sbk = DSA_UNROLL * BLK
    pad = (-ka.shape[1]) % sbk
    ka, ki, va = (jnp.pad(a, ((0, 0), (0, pad), (0, 0))) for a in (ka, ki, va))
    nk = ka.shape[1]
    nsb_total = nk // sbk
    vat = va.reshape(b, nsb_total, sbk, HEAD_DIM).transpose(0, 1, 3, 2)
    vat = jnp.concatenate([vat, jnp.ones((b, nsb_total, DENOM_ROWS, sbk), vat.dtype)], axis=2)
    nh = N_HEADS_A
    kernel = functools.partial(_dsa_kernel, qb0=qb0, nsb_total=nsb_total, n_valid=n_valid,
                               chunk_off=chunk_off, topk=topk)
    qspec = lambda w: pl.BlockSpec((None, BLK, w), lambda bi, qi_: (bi, qi_, 0))
    full = lambda a: pl.BlockSpec((None,) + a.shape[1:], lambda bi, qi_, nd=a.ndim: (bi,) + (0,) * (nd - 1))
    const = lambda a: pl.BlockSpec(a.shape, lambda bi, qi_, nd=a.ndim: (0,) * nd)
    return pl.pallas_call(
        kernel, grid=(b, nq // BLK),
        in_specs=[qspec(W_A), qspec(W_IDX_Q), qspec(2 * BLK), full(ka), full(ki), full(vat), const(rbt), const(bidx)],
        out_specs=qspec(W_A),
        out_shape=jax.ShapeDtypeStruct((b, nq, W_A), MXU_DTYPE),
        scratch_shapes=[
            pltpu.VMEM((nk, BLK), jnp.int32),
            pltpu.VMEM((4, BLK, nh * BLK), F32),
            pltpu.VMEM((SUBLANES, nh * BLK), F32),
            pltpu.VMEM((HEAD_DIM + DENOM_ROWS, nh * BLK), F32),
            pltpu.VMEM((SUBLANES, BLK), F32),
            pltpu.VMEM((sbk, nh * BLK), F32),
            pltpu.VMEM((sbk, nh * BLK), F32),
        ],
        compiler_params=_cparams(("arbitrary", "arbitrary")), name="dsa")(qa, qi, kvk, ka, ki, vat, rbt, bidx)


def _sb_kernel(q_ref, k_ref, v_ref, o_ref, *, qb0):
    i = pl.program_id(1) + qb0
    row = lax.broadcasted_iota(jnp.int32, (BLK, BLK), 0)
    col = lax.broadcasted_iota(jnp.int32, (BLK, BLK), 1)
    tri = jnp.where(row > col, 1.0, 0.0).astype(MXU_DTYPE)
    before = col < row
    scale = SB_HEAD_DIM ** -0.5
    hd = SB_HEAD_DIM
    qs = [q_ref[:, h * hd:(h + 1) * hd] for h in range(N_HEADS_SB)]

    def block(j, later_blocks, acc, diag):
        off = pl.multiple_of(j * BLK, BLK)
        new_lb, new_acc = [], []
        for h in range(N_HEADS_SB):
            kb = k_ref[pl.ds(off, BLK), h * hd:(h + 1) * hd]
            vb = v_ref[pl.ds(off, BLK), h * hd:(h + 1) * hd]
            z = lax.dot_general(qs[h], kb, (((1,), (1,)), ((), ())), preferred_element_type=F32) * scale
            log_keep = -(jnp.maximum(z, 0.0) + jnp.log1p(jnp.exp(-jnp.abs(z))))
            if diag:
                log_keep = jnp.where(before, log_keep, 0.0)
            later = _split_dot(log_keep, tri) + later_blocks[h]
            a = jnp.exp(log_keep + z + later)
            if diag:
                a = jnp.where(before, a, 0.0)
            new_acc.append(acc[h] + _dot(a.astype(MXU_DTYPE), vb))
            new_lb.append(later_blocks[h] + jnp.sum(log_keep, axis=1, keepdims=True))
        return tuple(new_lb), tuple(new_acc)

    def worst(lb):
        return jnp.max(functools.reduce(jnp.maximum, lb))

    zeros_lb = tuple(jnp.zeros((BLK, 1), F32) for _ in range(N_HEADS_SB))
    zeros_acc = tuple(jnp.zeros((BLK, hd), F32) for _ in range(N_HEADS_SB))
    later_blocks, acc = block(i, zeros_lb, zeros_acc, True)

    def cond(c):
        j, w, _, _ = c
        return jnp.logical_and(j >= 0, w > SB_UNDERFLOW)

    def body(c):
        j, _, lb, ac = c
        lb, ac = block(j, lb, ac, False)
        return j - 1, worst(lb), lb, ac

    _, _, _, acc = lax.while_loop(cond, body, (i - 1, worst(later_blocks), later_blocks, acc))
    for h in range(N_HEADS_SB):
        o_ref[:, h * hd:(h + 1) * hd] = acc[h].astype(o_ref.dtype)


def _sb_call(q, k, v, *, qb0):
    b, nq, _ = q.shape
    nk = k.shape[1]
    kernel = functools.partial(_sb_kernel, qb0=qb0)
    kv_spec = pl.BlockSpec((None, nk, W_SB), lambda bi, qi_: (bi, 0, 0))
    q_spec = pl.BlockSpec((None, BLK, W_SB), lambda bi, qi_: (bi, qi_, 0))
    return pl.pallas_call(
        kernel, grid=(b, nq // BLK),
        in_specs=[q_spec, kv_spec, kv_spec], out_specs=q_spec,
        out_shape=jax.ShapeDtypeStruct((b, nq, W_SB), MXU_DTYPE),
        compiler_params=_cparams(("arbitrary", "arbitrary")), name="sb")(q, k, v)


def _merge_kernel(x_ref, ya_ref, ys_ref, g_ref, wpa_ref, wps_ref, wo_ref, o_ref):
    d = x_ref.shape[-1]
    g = g_ref[...]
    m = (jax.nn.sigmoid(g[:, :d]) * _dot(ya_ref[...], wpa_ref[...])
         + jax.nn.sigmoid(g[:, d:]) * _dot(ys_ref[...], wps_ref[...]))
    o_ref[...] = x_ref[...] + _dot(m.astype(MXU_DTYPE), wo_ref[...])


def _merge_call(x, ya, ys, g, wpa, wps, wo):
    m, d = x.shape
    tm = _row_tile(m)
    row = lambda w: pl.BlockSpec((tm, w), lambda i: (i, 0))
    const = lambda a: pl.BlockSpec(a.shape, lambda i, nd=a.ndim: (0,) * nd)
    return pl.pallas_call(
        _merge_kernel, grid=(m // tm,),
        in_specs=[row(d), row(ya.shape[1]), row(ys.shape[1]), row(g.shape[1]), const(wpa), const(wps), const(wo)],
        out_specs=row(d), out_shape=jax.ShapeDtypeStruct((m, d), F32),
        compiler_params=_cparams(("arbitrary",)), name="merge")(x, ya, ys, g, wpa, wps, wo)


def _ffn_kernel(x_ref, g_ref, st_ref, wa_ref, wu_ref, cw_ref, cb_ref, wd_ref, o_ref,
                h_scr, carry_scr, act_scr):
    t = pl.program_id(1)
    c = pl.program_id(2)
    tm = x_ref.shape[0]

    @pl.when(c == 0)
    def _():
        x = x_ref[...]
        h_scr[...] = _rms(x, g_ref[...]).astype(h_scr.dtype)
        o_ref[...] = x

    @pl.when(t == 0)
    def _():
        carry_scr[c] = st_ref[...]

    h = h_scr[...]
    a = _dot(h, wa_ref[...])
    u = _dot(h, wu_ref[...])
    w0, w1, w2 = cw_ref[0:1, :], cw_ref[1:2, :], cw_ref[2:3, :]
    bias = cb_ref[...]

    def gate(a2, a1, a0, uu):
        cv = bias + (w0 * a2 + w1 * a1 + w2 * a0)
        return cv * jax.nn.sigmoid(cv) * uu

    act_scr[...] = gate(pltpu.roll(a, 2, 0), pltpu.roll(a, 1, 0), a, u)
    prev = carry_scr[c]
    p2 = prev[SUBLANES - 2:SUBLANES - 1, :]
    p1 = prev[SUBLANES - 1:SUBLANES, :]
    top = a[0:SUBLANES]
    rid = lax.broadcasted_iota(jnp.int32, top.shape, 0)
    a1 = jnp.where(rid == 0, p1, pltpu.roll(top, 1, 0))
    a2 = jnp.where(rid == 0, p2, jnp.where(rid == 1, p1, pltpu.roll(top, 2, 0)))
    act_scr[0:SUBLANES, :] = gate(a2, a1, top, u[0:SUBLANES])
    carry_scr[c] = a[tm - SUBLANES:tm]
    o_ref[...] += _dot(act_scr[...].astype(MXU_DTYPE), wd_ref[...])


def _ffn_call(x, g, state, wa, wu, cw, cb, wd, n_chunks):
    b, n, d = x.shape
    dff = wa.shape[1]
    cwid = dff // n_chunks
    tm = _row_tile(n)
    return pl.pallas_call(
        _ffn_kernel, grid=(b, n // tm, n_chunks),
        in_specs=[
            pl.BlockSpec((None, tm, d), lambda bi, t, c: (bi, t, 0)),
            pl.BlockSpec((1, d), lambda bi, t, c: (0, 0)),
            pl.BlockSpec((None, SUBLANES, cwid), lambda bi, t, c: (bi, 0, c)),
            pl.BlockSpec((d, cwid), lambda bi, t, c: (0, c)),
            pl.BlockSpec((d, cwid), lambda bi, t, c: (0, c)),
            pl.BlockSpec((CONV_W, cwid), lambda bi, t, c: (0, c)),
            pl.BlockSpec((1, cwid), lambda bi, t, c: (0, c)),
            pl.BlockSpec((cwid, d), lambda bi, t, c: (c, 0)),
        ],
        out_specs=pl.BlockSpec((None, tm, d), lambda bi, t, c: (bi, t, 0)),
        out_shape=jax.ShapeDtypeStruct((b, n, d), F32),
        scratch_shapes=[
            pltpu.VMEM((tm, d), MXU_DTYPE),
            pltpu.VMEM((n_chunks, SUBLANES, cwid), F32),
            pltpu.VMEM((tm, cwid), F32),
        ],
        compiler_params=_cparams(("arbitrary", "arbitrary", "arbitrary")), name="ffn")(
            x, g, state, wa, wu, cw, cb, wd)


def _ffn_chunks(dff):
    return 2 if dff % (2 * BLK) == 0 else 1


def _layer(x, kv_prefix, conv_state, lw, *, n_new, qb0, n_valid, chunk_off, topk, rbt, bidx):
    b, nq, d = x.shape
    xf = x.reshape(b * nq, d)
    g_mix = lw["ln_mix_g"]
    qa, qi, kvk, ka, va, ki = _rowwise_call(
        _proj_a_kernel, xf, [g_mix, lw["w_a"], lw["qg"], lw["kg"], lw["seg"]],
        [(W_A, MXU_DTYPE), (W_IDX_Q, MXU_DTYPE), (2 * BLK, F32), (HEAD_DIM, MXU_DTYPE), (HEAD_DIM, MXU_DTYPE),
         (IDX_DIM, MXU_DTYPE)], "proj_a")
    qs, ks, vs, ksb, vsb = _rowwise_call(
        _proj_b_kernel, xf, [g_mix, lw["w_b"]],
        [(W_SB, MXU_DTYPE), (W_SB, F32), (W_SB, F32), (W_SB, MXU_DTYPE), (W_SB, MXU_DTYPE)], "proj_b")
    (gates,) = _rowwise_call(_proj_c_kernel, xf, [g_mix, lw["w_c"]], [(2 * d, F32)], "proj_c")

    r3 = lambda a: a.reshape(b, nq, a.shape[-1])
    qa, qi, kvk, ka, va, ki, qs, ks, vs, ksb, vsb = map(r3, (qa, qi, kvk, ka, va, ki, qs, ks, vs, ksb, vsb))

    if kv_prefix is None:
        ka_all, va_all, ki_all, ks_all, vs_all = ka, va, ki, ksb, vsb
    else:
        def join(prefix, new):
            cat = jnp.concatenate([prefix, new[:, :n_new]], axis=1)
            pad = (-cat.shape[1]) % BLK
            return jnp.pad(cat, ((0, 0), (0, pad), (0, 0)))
        ka_all, va_all, ki_all, ks_all, vs_all = (join(p, n_) for p, n_ in zip(kv_prefix, (ka, va, ki, ksb, vsb)))
    ya = _dsa_call(qa, qi, kvk, ka_all, ki_all, va_all, rbt, bidx,
                   qb0=qb0, n_valid=n_valid, chunk_off=chunk_off, topk=topk)
    ys = _sb_call(qs, ks_all, vs_all, qb0=qb0)
    x_mid = _merge_call(xf, ya.reshape(b * nq, W_A), ys.reshape(b * nq, W_SB), gates,
                        lw["w_pa"], lw["w_ps"], lw["w_o"]).reshape(b, nq, d)

    last = x_mid[:, n_new - (CONV_W - 1):n_new].reshape(b * (CONV_W - 1), d)
    (conv_rows,) = _rowwise_call(_proj_c_kernel, last, [lw["ln_ffn_g"], lw["w_up_a"]],
                                 [(lw["w_up_a"].shape[1], F32)], "conv_state")
    x_out = _ffn_call(x_mid, lw["ln_ffn_g"], conv_state, lw["w_up_a"], lw["w_up_u"], lw["conv_w"], lw["conv_b"],
                      lw["w_down"], _ffn_chunks(lw["w_up_a"].shape[1]))
    new_rows = dict(a_k=kvk[:, :n_new, 0:HEAD_DIM], a_v=kvk[:, :n_new, HEAD_DIM:2 * HEAD_DIM],
                    idx_k=kvk[:, :n_new, 2 * HEAD_DIM:2 * HEAD_DIM + IDX_DIM],
                    sb_k=ks[:, :n_new], sb_v=vs[:, :n_new],
                    conv=conv_rows.reshape(b, CONV_W - 1, -1))
    return x_out, new_rows


def kernel(x_prompt, x_sample, cache_a_k, cache_a_v, cache_idx_k, cache_sb_k, cache_sb_v, state_ffn_conv, meta_tokens, rel_bias, ln_mix_g, w_in, q_norm_g, k_norm_g, w_proj_a, w_proj_sb, w_out, ln_ffn_g, w_up, conv_w, conv_b, w_down):
    depth, d_model, _ = w_in.shape
    b_p, seq, _ = x_prompt.shape
    n_meta = meta_tokens.shape[0]
    b_s, n_s, _ = x_sample.shape
    past = cache_a_k.shape[2]
    d_ff = w_down.shape[1]
    n_p = n_meta + seq
    topk_p = min(TOPK_MAX, seq // 4)
    topk_s = min(TOPK_MAX, (past + n_s) // 4)
    assert n_meta <= CHUNK and CONV_W - 1 <= min(n_s, SUBLANES)

    sizes = (W_A, HEAD_DIM, HEAD_DIM, W_IDX_Q, IDX_DIM, N_IDX_HEADS, W_SB, W_SB, W_SB, d_model, d_model)
    offs = np.concatenate([[0], np.cumsum(sizes)])
    col = lambda k: w_in[:, :, offs[k]:offs[k + 1]]
    w_pad = jnp.zeros((depth, d_model, 2 * BLK - 3 * HEAD_DIM - N_IDX_HEADS), w_in.dtype)
    w_a = jnp.concatenate([col(0), col(3), col(1), col(2), col(4), col(5), w_pad], axis=-1).astype(MXU_DTYPE)
    w_b = w_in[:, :, offs[6]:offs[9]].astype(MXU_DTYPE)
    w_c = w_in[:, :, offs[9]:offs[11]].astype(MXU_DTYPE)
    seg_np = np.kron(np.eye(N_HEADS_A), np.full((HEAD_DIM, HEAD_DIM), 1.0 / HEAD_DIM))
    seg = jnp.asarray(seg_np, MXU_DTYPE)
    kg_pad = jnp.concatenate([k_norm_g, jnp.ones((depth, 2 * BLK - HEAD_DIM), k_norm_g.dtype)], axis=-1)
    rbt = rel_bias.T.astype(F32)
    bidx = jnp.asarray(_near_bucket_table())

    layers = []
    for l in range(depth):
        layers.append(dict(
            ln_mix_g=ln_mix_g[l][None], w_a=w_a[l], w_b=w_b[l], w_c=w_c[l],
            qg=jnp.tile(q_norm_g[l], N_HEADS_A)[None], kg=kg_pad[l][None], seg=seg,
            w_pa=w_proj_a[l].astype(MXU_DTYPE), w_ps=w_proj_sb[l].astype(MXU_DTYPE), w_o=w_out[l].astype(MXU_DTYPE),
            ln_ffn_g=ln_ffn_g[l][None], w_up_a=w_up[l][:, :d_ff].astype(MXU_DTYPE),
            w_up_u=w_up[l][:, d_ff:].astype(MXU_DTYPE), conv_w=conv_w[l], conv_b=conv_b[l][None],
            w_down=w_down[l].astype(MXU_DTYPE)))

    np_pad = -(-n_p // BLK) * BLK
    meta = jnp.broadcast_to(meta_tokens.astype(x_prompt.dtype)[None], (b_p, n_meta, d_model))
    xp = jnp.concatenate([meta, x_prompt, jnp.zeros((b_p, np_pad - n_p, d_model), x_prompt.dtype)], axis=1)
    zero_state = jnp.zeros((b_p, SUBLANES, d_ff), F32)

    ns_pad = -(-n_s // BLK) * BLK
    assert past % BLK == 0 and ns_pad == BLK
    xs = jnp.pad(x_sample, ((0, 0), (0, ns_pad - n_s), (0, 0)))

    outs_p, outs_s = [], []
    for l in range(depth):
        lw = layers[l]
        xp, rows_p = _layer(xp, None, zero_state, lw, n_new=n_p, qb0=0, n_valid=n_p, chunk_off=n_meta,
                            topk=topk_p, rbt=rbt, bidx=bidx)
        outs_p.append(rows_p)

        prefix = (cache_a_k[l].reshape(b_s, past, HEAD_DIM).astype(MXU_DTYPE),
                  cache_a_v[l].reshape(b_s, past, HEAD_DIM).astype(MXU_DTYPE),
                  cache_idx_k[l].astype(MXU_DTYPE),
                  cache_sb_k[l].reshape(b_s, past, W_SB).astype(MXU_DTYPE),
                  cache_sb_v[l].reshape(b_s, past, W_SB).astype(MXU_DTYPE))
        st = jnp.pad(state_ffn_conv[l].astype(F32), ((0, 0), (SUBLANES - (CONV_W - 1), 0), (0, 0)))
        xs, rows_s = _layer(xs, prefix, st, lw, n_new=n_s, qb0=past // BLK, n_valid=past + n_s, chunk_off=0,
                            topk=topk_s, rbt=rbt, bidx=bidx)
        outs_s.append(rows_s)

    def stack(outs, name, shape_tail):
        a = jnp.stack([o[name] for o in outs])
        return a.reshape(a.shape[:3] + shape_tail)

    def group(outs):
        return (stack(outs, "a_k", (1, HEAD_DIM)), stack(outs, "a_v", (1, HEAD_DIM)), stack(outs, "idx_k", (IDX_DIM,)),
                stack(outs, "sb_k", (N_HEADS_SB, SB_HEAD_DIM)), stack(outs, "sb_v", (N_HEADS_SB, SB_HEAD_DIM)),
                stack(outs, "conv", (d_ff,)))

    y_prompt = xp[:, n_meta:n_p]
    y_sample = xs[:, :n_s]
    return (y_prompt, y_sample) + group(outs_p) + group(outs_s)
```

```python
import functools
import math

import numpy as np
import jax
import jax.numpy as jnp
from jax import lax
from jax.experimental import pallas as pl
from jax.experimental.pallas import tpu as pltpu

CHUNK = 64
HEAD_DIM = 64
N_HEADS_A = 8
N_IDX_HEADS = 4
IDX_DIM = 64
TOPK_MAX = 256
N_HEADS_SB = 4
SB_HEAD_DIM = 128
N_BUCKETS = 32
MAX_DISTANCE = 128
CONV_W = 3
EPS = 1e-6

W_A = N_HEADS_A * HEAD_DIM
W_IDX_Q = N_IDX_HEADS * IDX_DIM
W_SB = N_HEADS_SB * SB_HEAD_DIM

BLK = 128
DSA_UNROLL = 4
DENOM_ROWS = 16
LOG2E = math.log2(math.e)
SUBLANES = 8
ROW_TILE_MAX = 640
VMEM_LIMIT = 56 * 1024 * 1024
MXU_DTYPE = jnp.bfloat16
INT_MIN = -2 ** 31
SB_UNDERFLOW = -104.0

F32 = jnp.float32


def _cparams(sem):
    return pltpu.CompilerParams(dimension_semantics=sem, vmem_limit_bytes=VMEM_LIMIT)


def _row_tile(n_rows, cap=ROW_TILE_MAX):
    if n_rows <= BLK:
        return n_rows
    best = BLK
    t = BLK
    while t <= min(cap, n_rows):
        if n_rows % t == 0:
            best = t
        t += BLK
    return best


def _rms(x, g):
    return x * lax.rsqrt(jnp.mean(x * x, axis=-1, keepdims=True) + EPS) * g


def _dot(a, b):
    return jnp.dot(a, b, preferred_element_type=F32)


def _split_dot(a, b):
    hi = a.astype(MXU_DTYPE)
    lo = (a - hi.astype(F32)).astype(MXU_DTYPE)
    return _dot(hi, b) + _dot(lo, b)


def _proj_a_kernel(x_ref, g_ref, w_ref, qg_ref, kg_ref, seg_ref,
                   qa_ref, qi_ref, kvk_ref, ka_ref, va_ref, ki_ref):
    xn = _rms(x_ref[...], g_ref[...])
    y = _dot(xn.astype(MXU_DTYPE), w_ref[...])
    q = y[:, :W_A]
    ms = _split_dot(q * q, seg_ref[...])
    qa_ref[...] = (q * lax.rsqrt(ms + EPS) * qg_ref[...]).astype(qa_ref.dtype)
    qi_ref[...] = y[:, W_A:W_A + W_IDX_Q].astype(qi_ref.dtype)
    kvk = y[:, W_A + W_IDX_Q:]
    lane = lax.broadcasted_iota(jnp.int32, kvk.shape, 1)
    is_k = lane < HEAD_DIM
    msk = jnp.sum(jnp.where(is_k, kvk * kvk, 0.0), axis=-1, keepdims=True) * (1.0 / HEAD_DIM)
    kvk = jnp.where(is_k, kvk * lax.rsqrt(msk + EPS) * kg_ref[...], kvk)
    kvk_ref[...] = kvk
    ka_ref[...] = kvk[:, 0:HEAD_DIM].astype(ka_ref.dtype)
    va_ref[...] = kvk[:, HEAD_DIM:2 * HEAD_DIM].astype(va_ref.dtype)
    ki_ref[...] = kvk[:, 2 * HEAD_DIM:2 * HEAD_DIM + IDX_DIM].astype(ki_ref.dtype)


def _proj_b_kernel(x_ref, g_ref, w_ref, qs_ref, ks_ref, vs_ref, ksb_ref, vsb_ref):
    xn = _rms(x_ref[...], g_ref[...])
    y = _dot(xn.astype(MXU_DTYPE), w_ref[...])
    qs_ref[...] = y[:, :W_SB].astype(qs_ref.dtype)
    k = y[:, W_SB:2 * W_SB]
    v = y[:, 2 * W_SB:]
    ks_ref[...] = k
    vs_ref[...] = v
    ksb_ref[...] = k.astype(ksb_ref.dtype)
    vsb_ref[...] = v.astype(vsb_ref.dtype)


def _proj_c_kernel(x_ref, g_ref, w_ref, o_ref):
    xn = _rms(x_ref[...], g_ref[...])
    o_ref[...] = _dot(xn.astype(MXU_DTYPE), w_ref[...])


def _rowwise_call(kernel, x, consts, out_cols_dtypes, name):
    m, d = x.shape
    tm = _row_tile(m)
    in_specs = [pl.BlockSpec((tm, d), lambda i: (i, 0))]
    for c in consts:
        in_specs.append(pl.BlockSpec(c.shape, lambda i, nd=c.ndim: (0,) * nd))
    out_shape = [jax.ShapeDtypeStruct((m, n), dt) for n, dt in out_cols_dtypes]
    out_specs = [pl.BlockSpec((tm, n), lambda i: (i, 0)) for n, _ in out_cols_dtypes]
    return pl.pallas_call(
        kernel, grid=(m // tm,), in_specs=in_specs, out_specs=out_specs, out_shape=out_shape,
        compiler_params=_cparams(("arbitrary",)), name=name)(x, *consts)


def _t5_bucket_np(rel):
    half = N_BUCKETS // 2
    max_exact = half // 2
    n = np.abs(rel)
    large = np.full(n.shape, max_exact, dtype=np.int64)
    steps = half - max_exact
    for t in range(1, steps + 1):
        lhs = n.astype(object) ** steps
        rhs = (max_exact ** steps) * ((MAX_DISTANCE // max_exact) ** t)
        large = large + (np.array(lhs >= rhs, dtype=bool)).astype(np.int64)
    large = np.minimum(large, half - 1)
    return np.where(rel > 0, half, 0) + np.where(n < max_exact, n, large)


def _near_bucket_table():
    m = np.arange(2 * BLK)
    rows = [_t5_bucket_np(d * BLK + BLK - m) for d in (-1, 0, 1)]
    return np.stack(rows).astype(np.int32)


def _pair_transpose(x, n_pairs):
    cols = []
    for p in range(n_pairs):
        t = x[:, p * BLK:(p + 1) * BLK].T
        cols.append(t[:HEAD_DIM])
        cols.append(t[HEAD_DIM:])
    return jnp.concatenate(cols, axis=1)


def _dsa_kernel(qa_ref, qi_ref, kvk_ref, ka_ref, ki_ref, vat_ref, rbt_ref, bidx_ref, o_ref,
                key_scr, bias_scr, m_scr, acc_scr, tie_scr, lg0_scr, lg1_scr, cm0_scr, cm1_scr,
                *, qb0, nsb_total, n_valid, chunk_off, topk):
    i = pl.program_id(1) + qb0
    nsb = jnp.minimum((i + 2 + DSA_UNROLL - 1) // DSA_UNROLL, nsb_total)
    nh = N_HEADS_A
    sbk = DSA_UNROLL * BLK

    qat = (_pair_transpose(qa_ref[...].astype(F32), nh // 2) * (HEAD_DIM ** -0.5 * LOG2E)).astype(MXU_DTYPE)
    qit = _pair_transpose(qi_ref[...].astype(F32), N_IDX_HEADS // 2).astype(MXU_DTYPE)
    kw_t = kvk_ref[...][:, BLK:2 * BLK].T
    w_scale = (IDX_DIM ** -0.5) * (N_IDX_HEADS ** -0.5)
    w_rows = [kw_t[IDX_DIM + h:IDX_DIM + h + 1, :] * w_scale for h in range(N_IDX_HEADS)]

    qpos = i * BLK + lax.broadcasted_iota(jnp.int32, (1, BLK), 1)
    kend = (((qpos + (CHUNK - chunk_off)) >> 6) << 6) + chunk_off
    kend = jnp.minimum(kend, n_valid)
    krow = lax.broadcasted_iota(jnp.int32, (sbk, BLK), 0)

    @pl.when(jnp.logical_and(pl.program_id(0) == 0, pl.program_id(1) == 0))
    def _():
        rbt = rbt_ref[...]
        far_bucket = N_BUCKETS // 2 - 1
        cfar = rbt[:, far_bucket:far_bucket + 1]
        bias_scr[0] = jnp.zeros((BLK, nh * BLK), F32)
        for d in range(3):
            idx = bidx_ref[d:d + 1, :]
            tab = jnp.zeros((nh, 2 * BLK), F32)
            for b in range(N_BUCKETS):
                tab = jnp.where(idx == b, rbt[:, b:b + 1], tab)
            tab = (tab - cfar) * LOG2E
            for h in range(nh):
                trow = jnp.broadcast_to(tab[h:h + 1, :], (BLK, 2 * BLK))
                bias_scr[d + 1, :, h * BLK:(h + 1) * BLK] = pltpu.roll(trow, 0, 1, stride=1, stride_axis=0)[:, BLK:]

    def to_key(v):
        bits = lax.bitcast_convert_type(v, jnp.int32)
        key = bits ^ ((bits >> 31) & 0x7FFFFFFF)
        return jnp.where(v == 0.0, 0, key)

    def from_key(key):
        return lax.bitcast_convert_type(key ^ ((key >> 31) & 0x7FFFFFFF), F32)

    def score_block(s_, carry):
        smin, smax = carry
        off = pl.multiple_of(s_ * sbk, sbk)
        s = jnp.maximum(_dot(ki_ref[pl.ds(off, sbk), :], qit), 0.0)
        sc = s[:, 0:BLK] * w_rows[0]
        for h in range(1, N_IDX_HEADS):
            sc = sc + s[:, h * BLK:(h + 1) * BLK] * w_rows[h]
        adm = (krow + off) < kend
        smin = jnp.minimum(smin, jnp.min(jnp.where(adm, sc, jnp.inf), axis=0, keepdims=True))
        smax = jnp.maximum(smax, jnp.max(jnp.where(adm, sc, -jnp.inf), axis=0, keepdims=True))
        key_scr[pl.ds(off, sbk), :] = jnp.where(adm, to_key(sc), INT_MIN)
        return smin, smax

    smin, smax = lax.fori_loop(0, nsb, score_block,
                               (jnp.full((1, BLK), jnp.inf, F32), jnp.full((1, BLK), -jnp.inf, F32)))

    n_acc = 8 * SUBLANES

    def count(thrs):
        def body(s_, accs):
            off = pl.multiple_of(s_ * sbk, sbk)
            blk = key_scr[pl.ds(off, sbk), :]
            return tuple(acc + jnp.sum(jnp.where(blk >= t, 1.0, 0.0).reshape(sbk // n_acc, n_acc, BLK), axis=0)
                         for acc, t in zip(accs, thrs))
        accs = lax.fori_loop(0, nsb, body, tuple(jnp.zeros((n_acc, BLK), F32) for _ in thrs))
        return [jnp.sum(acc, axis=0, keepdims=True) for acc in accs]

    kf = float(topk)
    n_adm = kend.astype(F32)
    few = n_adm < kf

    def narrow(state, cand, cnt, live):
        lo, c_lo, hi, c_hi = state
        up = cnt >= kf
        take_lo = live & up & (cand > lo)
        take_hi = live & jnp.logical_not(up) & (cand < hi)
        return (jnp.where(take_lo, cand, lo), jnp.where(take_lo, cnt, c_lo),
                jnp.where(take_hi, cand, hi), jnp.where(take_hi, cnt, c_hi))

    def is_open(state):
        lo, c_lo, hi, c_hi = state
        return jnp.logical_not(few) & (c_lo != kf) & (hi - 1 > lo) & (c_lo - c_hi > 2.0)

    state = (to_key(smin), n_adm, to_key(smax) + 1, jnp.zeros((1, BLK), F32))
    zero = jnp.zeros((1, BLK), jnp.int32)
    c_nonneg, c_pos = count([zero, zero + 1])
    all_lanes = jnp.logical_not(few)
    state = narrow(state, zero, c_nonneg, all_lanes)
    state = narrow(state, zero + 1, c_pos, all_lanes)

    group = 2
    max_passes = 4 * 33

    def n_open_of(state):
        return jnp.sum(jnp.where(is_open(state), 1.0, 0.0))

    def search_cond(c):
        t, n_open = c[0], c[1]
        return jnp.logical_and(t < max_passes, n_open > 0.0)

    def search_steps(c):
        t, _, state = c[0], c[1], c[2:]
        for u in range(group):
            lo, _, hi, _ = state
            mid_v = to_key(0.5 * from_key(lo) + 0.5 * from_key(hi - 1))
            mid_k = (lo >> 1) + (hi >> 1) + (lo & hi & 1)
            cand = jnp.where((t + u) % 4 == 3, mid_k, mid_v)
            cand = jnp.minimum(jnp.maximum(cand, lo + 1), hi - 1)
            (cnt,) = count([cand])
            state = narrow(state, cand, cnt, is_open(state))
        return (t + group, n_open_of(state)) + tuple(state)

    res = lax.while_loop(search_cond, search_steps, (jnp.int32(0), n_open_of(state)) + tuple(state))
    lo, c_lo, hi, c_hi = res[2:]
    exact_k = c_lo == kf
    exact_t = jnp.logical_not(exact_k) & jnp.logical_not(hi - 1 > lo)
    pair = jnp.logical_not(few | exact_k | exact_t)

    def largest_below(bound):
        def body(s_, acc):
            off = pl.multiple_of(s_ * sbk, sbk)
            blk = key_scr[pl.ds(off, sbk), :]
            return jnp.maximum(acc, jnp.max(jnp.where(blk < bound, blk, INT_MIN).reshape(sbk // n_acc, n_acc, BLK),
                                            axis=0))
        acc = lax.fori_loop(0, nsb, body, jnp.full((n_acc, BLK), INT_MIN, jnp.int32))
        rows = n_acc
        while rows > SUBLANES:
            rows //= 2
            acc = jnp.maximum(acc[:rows], acc[rows:2 * rows])
        for shift in (4, 2, 1):
            acc = jnp.maximum(acc, pltpu.roll(acc, shift, 0))
        return acc[0:1]

    kth = lax.cond(jnp.sum(jnp.where(pair, 1.0, 0.0)) > 0.0, lambda: largest_below(hi), lambda: lo)
    thr = jnp.where(few, INT_MIN, jnp.where(exact_k, lo - 1, jnp.where(exact_t, lo, kth)))
    n_tie_keep = jnp.where(few | exact_k, 0.0, kf - c_hi)

    m_scr[...] = jnp.full(m_scr.shape, -jnp.inf, F32)
    acc_scr[...] = jnp.zeros(acc_scr.shape, F32)
    tie_scr[...] = jnp.zeros(tie_scr.shape, F32)
    lrow = lax.broadcasted_iota(jnp.int32, (sbk, sbk), 0)
    lcol = lax.broadcasted_iota(jnp.int32, (sbk, sbk), 1)
    lstrict = jnp.where(lcol < lrow, 1.0, 0.0).astype(MXU_DTYPE)

    def stage_a(s_next, lg_ref, cm_ref):
        s_ = jnp.minimum(s_next, nsb - 1)
        off = pl.multiple_of(s_ * sbk, sbk)
        key = key_scr[pl.ds(off, sbk), :]
        eq = key == thr
        eqf = jnp.where(eq, 1.0, 0.0)
        ties_before = _dot(lstrict, eqf.astype(MXU_DTYPE)) + tie_scr[0:1, :]
        sel = (key > thr) | (eq & (ties_before < n_tie_keep))
        tie_scr[0:1, :] = tie_scr[0:1, :] + jnp.sum(eqf, axis=0, keepdims=True)
        selb = jnp.where(sel, jnp.where(s_next < nsb, 0.0, -jnp.inf), -jnp.inf)
        kab = ka_ref[pl.ds(off, sbk), :]
        tile_idx = [jnp.clip(s_ * DSA_UNROLL + u - i, -2, 1) + 2 for u in range(DSA_UNROLL)]
        for hp in range(nh // 2):
            lg = _dot(kab, qat[:, hp * 2 * BLK:(hp + 1) * 2 * BLK])
            for hh in range(2):
                h = 2 * hp + hh
                cols = slice(h * BLK, (h + 1) * BLK)
                col_max = None
                for u in range(DSA_UNROLL):
                    rows = slice(u * BLK, (u + 1) * BLK)
                    piece = lg[rows, hh * BLK:(hh + 1) * BLK] + bias_scr[tile_idx[u], :, cols] + selb[rows, :]
                    lg_ref[rows, cols] = piece
                    part = jnp.max(piece.reshape(BLK // SUBLANES, SUBLANES, BLK), axis=0)
                    col_max = part if col_max is None else jnp.maximum(col_max, part)
                cm_ref[:, cols] = col_max

    def stage_b(s_, lg_ref, cm_ref):
        m_old = m_scr[0:1, :]
        m_new = jnp.maximum(m_old, jnp.max(cm_ref[...], axis=0, keepdims=True))
        m_safe = jnp.where(m_new == -jnp.inf, 0.0, m_new)
        alpha = jnp.exp2(m_old - m_safe)
        p = jnp.exp2(lg_ref[...] - m_safe)
        vat = vat_ref[jnp.minimum(s_, nsb_total - 1)]
        acc_scr[...] = acc_scr[...] * alpha + _dot(vat, p.astype(MXU_DTYPE))
        m_scr[0:1, :] = m_new

    def pair_body(pi, carry):
        s_ = 2 * pi
        stage_a(s_ + 1, lg1_scr, cm1_scr)
        stage_b(s_, lg0_scr, cm0_scr)
        stage_a(s_ + 2, lg0_scr, cm0_scr)
        stage_b(s_ + 1, lg1_scr, cm1_scr)
        return carry

    stage_a(0, lg0_scr, cm0_scr)
    lax.fori_loop(0, (nsb + 1) // 2, pair_body, 0)

    out_t = acc_scr[0:HEAD_DIM, :] / acc_scr[HEAD_DIM:HEAD_DIM + 1, :]
    for p in range(nh // 2):
        pair = jnp.concatenate([out_t[:, (2 * p) * BLK:(2 * p + 1) * BLK],
                                out_t[:, (2 * p + 1) * BLK:(2 * p + 2) * BLK]], axis=0)
        o_ref[:, p * BLK:(p + 1) * BLK] = pair.T.astype(o_ref.dtype)


def _dsa_call(qa, qi, kvk, ka, ki, va, rbt, bidx, *, qb0, n_valid, chunk_off, topk):
    b, nq, _ = qa.shape
    sbk = DSA_UNROLL * BLK
    pad = (-ka.shape[1]) % sbk
    ka, ki, va = (jnp.pad(a, ((0, 0), (0, pad), (0, 0))) for a in (ka, ki, va))
    nk = ka.shape[1]
    nsb_total = nk // sbk
    vat = va.reshape(b, nsb_total, sbk, HEAD_DIM).transpose(0, 1, 3, 2)
    vat = jnp.concatenate([vat, jnp.ones((b, nsb_total, DENOM_ROWS, sbk), vat.dtype)], axis=2)
    nh = N_HEADS_A
    kernel = functools.partial(_dsa_kernel, qb0=qb0, nsb_total=nsb_total, n_valid=n_valid,
                               chunk_off=chunk_off, topk=topk)
    qspec = lambda w: pl.BlockSpec((None, BLK, w), lambda bi, qi_: (bi, qi_, 0))
    full = lambda a: pl.BlockSpec((None,) + a.shape[1:], lambda bi, qi_, nd=a.ndim: (bi,) + (0,) * (nd - 1))
    const = lambda a: pl.BlockSpec(a.shape, lambda bi, qi_, nd=a.ndim: (0,) * nd)
    return pl.pallas_call(
        kernel, grid=(b, nq // BLK),
        in_specs=[qspec(W_A), qspec(W_IDX_Q), qspec(2 * BLK), full(ka), full(ki), full(vat), const(rbt), const(bidx)],
        out_specs=qspec(W_A),
        out_shape=jax.ShapeDtypeStruct((b, nq, W_A), MXU_DTYPE),
        scratch_shapes=[
            pltpu.VMEM((nk, BLK), jnp.int32),
            pltpu.VMEM((4, BLK, nh * BLK), F32),
            pltpu.VMEM((SUBLANES, nh * BLK), F32),
            pltpu.VMEM((HEAD_DIM + DENOM_ROWS, nh * BLK), F32),
            pltpu.VMEM((SUBLANES, BLK), F32),
            pltpu.VMEM((sbk, nh * BLK), F32),
            pltpu.VMEM((sbk, nh * BLK), F32),
            pltpu.VMEM((SUBLANES, nh * BLK), F32),
            pltpu.VMEM((SUBLANES, nh * BLK), F32),
        ],
        compiler_params=_cparams(("arbitrary", "arbitrary")), name="dsa")(qa, qi, kvk, ka, ki, vat, rbt, bidx)


def _sb_kernel(q_ref, k_ref, v_ref, o_ref, *, qb0):
    i = pl.program_id(1) + qb0
    row = lax.broadcasted_iota(jnp.int32, (BLK, BLK), 0)
    col = lax.broadcasted_iota(jnp.int32, (BLK, BLK), 1)
    tri = jnp.where(row > col, 1.0, 0.0).astype(MXU_DTYPE)
    before = col < row
    scale = SB_HEAD_DIM ** -0.5
    hd = SB_HEAD_DIM
    qs = [q_ref[:, h * hd:(h + 1) * hd] for h in range(N_HEADS_SB)]

    def block(j, later_blocks, acc, diag):
        off = pl.multiple_of(j * BLK, BLK)
        new_lb, new_acc = [], []
        for h in range(N_HEADS_SB):
            kb = k_ref[pl.ds(off, BLK), h * hd:(h + 1) * hd]
            vb = v_ref[pl.ds(off, BLK), h * hd:(h + 1) * hd]
            z = lax.dot_general(qs[h], kb, (((1,), (1,)), ((), ())), preferred_element_type=F32) * scale
            log_keep = -(jnp.maximum(z, 0.0) + jnp.log1p(jnp.exp(-jnp.abs(z))))
            if diag:
                log_keep = jnp.where(before, log_keep, 0.0)
            later = _split_dot(log_keep, tri) + later_blocks[h]
            a = jnp.exp(log_keep + z + later)
            if diag:
                a = jnp.where(before, a, 0.0)
            new_acc.append(acc[h] + _dot(a.astype(MXU_DTYPE), vb))
            new_lb.append(later_blocks[h] + jnp.sum(log_keep, axis=1, keepdims=True))
        return tuple(new_lb), tuple(new_acc)

    def worst(lb):
        return jnp.max(functools.reduce(jnp.maximum, lb))

    zeros_lb = tuple(jnp.zeros((BLK, 1), F32) for _ in range(N_HEADS_SB))
    zeros_acc = tuple(jnp.zeros((BLK, hd), F32) for _ in range(N_HEADS_SB))
    later_blocks, acc = block(i, zeros_lb, zeros_acc, True)

    def cond(c):
        j, w, _, _ = c
        return jnp.logical_and(j >= 0, w > SB_UNDERFLOW)

    def body(c):
        j, _, lb, ac = c
        lb, ac = block(j, lb, ac, False)
        return j - 1, worst(lb), lb, ac

    _, _, _, acc = lax.while_loop(cond, body, (i - 1, worst(later_blocks), later_blocks, acc))
    for h in range(N_HEADS_SB):
        o_ref[:, h * hd:(h + 1) * hd] = acc[h].astype(o_ref.dtype)


def _sb_call(q, k, v, *, qb0):
    b, nq, _ = q.shape
    nk = k.shape[1]
    kernel = functools.partial(_sb_kernel, qb0=qb0)
    kv_spec = pl.BlockSpec((None, nk, W_SB), lambda bi, qi_: (bi, 0, 0))
    q_spec = pl.BlockSpec((None, BLK, W_SB), lambda bi, qi_: (bi, qi_, 0))
    return pl.pallas_call(
        kernel, grid=(b, nq // BLK),
        in_specs=[q_spec, kv_spec, kv_spec], out_specs=q_spec,
        out_shape=jax.ShapeDtypeStruct((b, nq, W_SB), MXU_DTYPE),
        compiler_params=_cparams(("arbitrary", "arbitrary")), name="sb")(q, k, v)


def _merge_kernel(x_ref, ya_ref, ys_ref, g_ref, wpa_ref, wps_ref, wo_ref, o_ref):
    d = x_ref.shape[-1]
    g = g_ref[...]
    m = (jax.nn.sigmoid(g[:, :d]) * _dot(ya_ref[...], wpa_ref[...])
         + jax.nn.sigmoid(g[:, d:]) * _dot(ys_ref[...], wps_ref[...]))
    o_ref[...] = x_ref[...] + _dot(m.astype(MXU_DTYPE), wo_ref[...])


def _merge_call(x, ya, ys, g, wpa, wps, wo):
    m, d = x.shape
    tm = _row_tile(m)
    row = lambda w: pl.BlockSpec((tm, w), lambda i: (i, 0))
    const = lambda a: pl.BlockSpec(a.shape, lambda i, nd=a.ndim: (0,) * nd)
    return pl.pallas_call(
        _merge_kernel, grid=(m // tm,),
        in_specs=[row(d), row(ya.shape[1]), row(ys.shape[1]), row(g.shape[1]), const(wpa), const(wps), const(wo)],
        out_specs=row(d), out_shape=jax.ShapeDtypeStruct((m, d), F32),
        compiler_params=_cparams(("arbitrary",)), name="merge")(x, ya, ys, g, wpa, wps, wo)


def _ffn_kernel(x_ref, g_ref, st_ref, wa_ref, wu_ref, cw_ref, cb_ref, wd_ref, o_ref,
                h_scr, carry_scr, act_scr):
    t = pl.program_id(1)
    c = pl.program_id(2)
    tm = x_ref.shape[0]

    @pl.when(c == 0)
    def _():
        x = x_ref[...]
        h_scr[...] = _rms(x, g_ref[...]).astype(h_scr.dtype)
        o_ref[...] = x

    @pl.when(t == 0)
    def _():
        carry_scr[c] = st_ref[...]

    h = h_scr[...]
    a = _dot(h, wa_ref[...])
    u = _dot(h, wu_ref[...])
    w0, w1, w2 = cw_ref[0:1, :], cw_ref[1:2, :], cw_ref[2:3, :]
    bias = cb_ref[...]

    def gate(a2, a1, a0, uu):
        cv = bias + (w0 * a2 + w1 * a1 + w2 * a0)
        return cv * jax.nn.sigmoid(cv) * uu

    act_scr[...] = gate(pltpu.roll(a, 2, 0), pltpu.roll(a, 1, 0), a, u)
    prev = carry_scr[c]
    p2 = prev[SUBLANES - 2:SUBLANES - 1, :]
    p1 = prev[SUBLANES - 1:SUBLANES, :]
    top = a[0:SUBLANES]
    rid = lax.broadcasted_iota(jnp.int32, top.shape, 0)
    a1 = jnp.where(rid == 0, p1, pltpu.roll(top, 1, 0))
    a2 = jnp.where(rid == 0, p2, jnp.where(rid == 1, p1, pltpu.roll(top, 2, 0)))
    act_scr[0:SUBLANES, :] = gate(a2, a1, top, u[0:SUBLANES])
    carry_scr[c] = a[tm - SUBLANES:tm]
    o_ref[...] += _dot(act_scr[...].astype(MXU_DTYPE), wd_ref[...])


def _ffn_call(x, g, state, wa, wu, cw, cb, wd, n_chunks):
    b, n, d = x.shape
    dff = wa.shape[1]
    cwid = dff // n_chunks
    tm = _row_tile(n)
    return pl.pallas_call(
        _ffn_kernel, grid=(b, n // tm, n_chunks),
        in_specs=[
            pl.BlockSpec((None, tm, d), lambda bi, t, c: (bi, t, 0)),
            pl.BlockSpec((1, d), lambda bi, t, c: (0, 0)),
            pl.BlockSpec((None, SUBLANES, cwid), lambda bi, t, c: (bi, 0, c)),
            pl.BlockSpec((d, cwid), lambda bi, t, c: (0, c)),
            pl.BlockSpec((d, cwid), lambda bi, t, c: (0, c)),
            pl.BlockSpec((CONV_W, cwid), lambda bi, t, c: (0, c)),
            pl.BlockSpec((1, cwid), lambda bi, t, c: (0, c)),
            pl.BlockSpec((cwid, d), lambda bi, t, c: (c, 0)),
        ],
        out_specs=pl.BlockSpec((None, tm, d), lambda bi, t, c: (bi, t, 0)),
        out_shape=jax.ShapeDtypeStruct((b, n, d), F32),
        scratch_shapes=[
            pltpu.VMEM((tm, d), MXU_DTYPE),
            pltpu.VMEM((n_chunks, SUBLANES, cwid), F32),
            pltpu.VMEM((tm, cwid), F32),
        ],
        compiler_params=_cparams(("arbitrary", "arbitrary", "arbitrary")), name="ffn")(
            x, g, state, wa, wu, cw, cb, wd)


def _ffn_chunks(dff):
    return 2 if dff % (2 * BLK) == 0 else 1


def _layer(x, kv_prefix, conv_state, lw, *, n_new, qb0, n_valid, chunk_off, topk, rbt, bidx):
    b, nq, d = x.shape
    xf = x.reshape(b * nq, d)
    g_mix = lw["ln_mix_g"]
    qa, qi, kvk, ka, va, ki = _rowwise_call(
        _proj_a_kernel, xf, [g_mix, lw["w_a"], lw["qg"], lw["kg"], lw["seg"]],
        [(W_A, MXU_DTYPE), (W_IDX_Q, MXU_DTYPE), (2 * BLK, F32), (HEAD_DIM, MXU_DTYPE), (HEAD_DIM, MXU_DTYPE),
         (IDX_DIM, MXU_DTYPE)], "proj_a")
    qs, ks, vs, ksb, vsb = _rowwise_call(
        _proj_b_kernel, xf, [g_mix, lw["w_b"]],
        [(W_SB, MXU_DTYPE), (W_SB, F32), (W_SB, F32), (W_SB, MXU_DTYPE), (W_SB, MXU_DTYPE)], "proj_b")
    (gates,) = _rowwise_call(_proj_c_kernel, xf, [g_mix, lw["w_c"]], [(2 * d, F32)], "proj_c")

    r3 = lambda a: a.reshape(b, nq, a.shape[-1])
    qa, qi, kvk, ka, va, ki, qs, ks, vs, ksb, vsb = map(r3, (qa, qi, kvk, ka, va, ki, qs, ks, vs, ksb, vsb))

    if kv_prefix is None:
        ka_all, va_all, ki_all, ks_all, vs_all = ka, va, ki, ksb, vsb
    else:
        def join(prefix, new):
            cat = jnp.concatenate([prefix, new[:, :n_new]], axis=1)
            pad = (-cat.shape[1]) % BLK
            return jnp.pad(cat, ((0, 0), (0, pad), (0, 0)))
        ka_all, va_all, ki_all, ks_all, vs_all = (join(p, n_) for p, n_ in zip(kv_prefix, (ka, va, ki, ksb, vsb)))
    ya = _dsa_call(qa, qi, kvk, ka_all, ki_all, va_all, rbt, bidx,
                   qb0=qb0, n_valid=n_valid, chunk_off=chunk_off, topk=topk)
    ys = _sb_call(qs, ks_all, vs_all, qb0=qb0)
    x_mid = _merge_call(xf, ya.reshape(b * nq, W_A), ys.reshape(b * nq, W_SB), gates,
                        lw["w_pa"], lw["w_ps"], lw["w_o"]).reshape(b, nq, d)

    last = x_mid[:, n_new - (CONV_W - 1):n_new].reshape(b * (CONV_W - 1), d)
    (conv_rows,) = _rowwise_call(_proj_c_kernel, last, [lw["ln_ffn_g"], lw["w_up_a"]],
                                 [(lw["w_up_a"].shape[1], F32)], "conv_state")
    x_out = _ffn_call(x_mid, lw["ln_ffn_g"], conv_state, lw["w_up_a"], lw["w_up_u"], lw["conv_w"], lw["conv_b"],
                      lw["w_down"], _ffn_chunks(lw["w_up_a"].shape[1]))
    new_rows = dict(a_k=kvk[:, :n_new, 0:HEAD_DIM], a_v=kvk[:, :n_new, HEAD_DIM:2 * HEAD_DIM],
                    idx_k=kvk[:, :n_new, 2 * HEAD_DIM:2 * HEAD_DIM + IDX_DIM],
                    sb_k=ks[:, :n_new], sb_v=vs[:, :n_new],
                    conv=conv_rows.reshape(b, CONV_W - 1, -1))
    return x_out, new_rows


def kernel(x_prompt, x_sample, cache_a_k, cache_a_v, cache_idx_k, cache_sb_k, cache_sb_v, state_ffn_conv, meta_tokens, rel_bias, ln_mix_g, w_in, q_norm_g, k_norm_g, w_proj_a, w_proj_sb, w_out, ln_ffn_g, w_up, conv_w, conv_b, w_down):
    depth, d_model, _ = w_in.shape
    b_p, seq, _ = x_prompt.shape
    n_meta = meta_tokens.shape[0]
    b_s, n_s, _ = x_sample.shape
    past = cache_a_k.shape[2]
    d_ff = w_down.shape[1]
    n_p = n_meta + seq
    topk_p = min(TOPK_MAX, seq // 4)
    topk_s = min(TOPK_MAX, (past + n_s) // 4)
    assert n_meta <= CHUNK and CONV_W - 1 <= min(n_s, SUBLANES)

    sizes = (W_A, HEAD_DIM, HEAD_DIM, W_IDX_Q, IDX_DIM, N_IDX_HEADS, W_SB, W_SB, W_SB, d_model, d_model)
    offs = np.concatenate([[0], np.cumsum(sizes)])
    col = lambda k: w_in[:, :, offs[k]:offs[k + 1]]
    w_pad = jnp.zeros((depth, d_model, 2 * BLK - 3 * HEAD_DIM - N_IDX_HEADS), w_in.dtype)
    w_a = jnp.concatenate([col(0), col(3), col(1), col(2), col(4), col(5), w_pad], axis=-1).astype(MXU_DTYPE)
    w_b = w_in[:, :, offs[6]:offs[9]].astype(MXU_DTYPE)
    w_c = w_in[:, :, offs[9]:offs[11]].astype(MXU_DTYPE)
    seg_np = np.kron(np.eye(N_HEADS_A), np.full((HEAD_DIM, HEAD_DIM), 1.0 / HEAD_DIM))
    seg = jnp.asarray(seg_np, MXU_DTYPE)
    kg_pad = jnp.concatenate([k_norm_g, jnp.ones((depth, 2 * BLK - HEAD_DIM), k_norm_g.dtype)], axis=-1)
    rbt = rel_bias.T.astype(F32)
    bidx = jnp.asarray(_near_bucket_table())

    layers = []
    for l in range(depth):
        layers.append(dict(
            ln_mix_g=ln_mix_g[l][None], w_a=w_a[l], w_b=w_b[l], w_c=w_c[l],
            qg=jnp.tile(q_norm_g[l], N_HEADS_A)[None], kg=kg_pad[l][None], seg=seg,
            w_pa=w_proj_a[l].astype(MXU_DTYPE), w_ps=w_proj_sb[l].astype(MXU_DTYPE), w_o=w_out[l].astype(MXU_DTYPE),
            ln_ffn_g=ln_ffn_g[l][None], w_up_a=w_up[l][:, :d_ff].astype(MXU_DTYPE),
            w_up_u=w_up[l][:, d_ff:].astype(MXU_DTYPE), conv_w=conv_w[l], conv_b=conv_b[l][None],
            w_down=w_down[l].astype(MXU_DTYPE)))

    np_pad = -(-n_p // BLK) * BLK
    meta = jnp.broadcast_to(meta_tokens.astype(x_prompt.dtype)[None], (b_p, n_meta, d_model))
    xp = jnp.concatenate([meta, x_prompt, jnp.zeros((b_p, np_pad - n_p, d_model), x_prompt.dtype)], axis=1)
    zero_state = jnp.zeros((b_p, SUBLANES, d_ff), F32)

    ns_pad = -(-n_s // BLK) * BLK
    assert past % BLK == 0 and ns_pad == BLK
    xs = jnp.pad(x_sample, ((0, 0), (0, ns_pad - n_s), (0, 0)))

    outs_p, outs_s = [], []
    for l in range(depth):
        lw = layers[l]
        xp, rows_p = _layer(xp, None, zero_state, lw, n_new=n_p, qb0=0, n_valid=n_p, chunk_off=n_meta,
                            topk=topk_p, rbt=rbt, bidx=bidx)
        outs_p.append(rows_p)

        prefix = (cache_a_k[l].reshape(b_s, past, HEAD_DIM).astype(MXU_DTYPE),
                  cache_a_v[l].reshape(b_s, past, HEAD_DIM).astype(MXU_DTYPE),
                  cache_idx_k[l].astype(MXU_DTYPE),
                  cache_sb_k[l].reshape(b_s, past, W_SB).astype(MXU_DTYPE),
                  cache_sb_v[l].reshape(b_s, past, W_SB).astype(MXU_DTYPE))
        st = jnp.pad(state_ffn_conv[l].astype(F32), ((0, 0), (SUBLANES - (CONV_W - 1), 0), (0, 0)))
        xs, rows_s = _layer(xs, prefix, st, lw, n_new=n_s, qb0=past // BLK, n_valid=past + n_s, chunk_off=0,
                            topk=topk_s, rbt=rbt, bidx=bidx)
        outs_s.append(rows_s)

    def stack(outs, name, shape_tail):
        a = jnp.stack([o[name] for o in outs])
        return a.reshape(a.shape[:3] + shape_tail)

    def group(outs):
        return (stack(outs, "a_k", (1, HEAD_DIM)), stack(outs, "a_v", (1, HEAD_DIM)), stack(outs, "idx_k", (IDX_DIM,)),
                stack(outs, "sb_k", (N_HEADS_SB, SB_HEAD_DIM)), stack(outs, "sb_v", (N_HEADS_SB, SB_HEAD_DIM)),
                stack(outs, "conv", (d_ff,)))

    y_prompt = xp[:, n_meta:n_p]
    y_sample = xs[:, :n_s]
    return (y_prompt, y_sample) + group(outs_p) + group(outs_s)
```

```python
import functools
import math

import numpy as np
import jax
import jax.numpy as jnp
from jax import lax
from jax.experimental import pallas as pl
from jax.experimental.pallas import tpu as pltpu

CHUNK = 64
HEAD_DIM = 64
N_HEADS_A = 8
N_IDX_HEADS = 4
IDX_DIM = 64
TOPK_MAX = 256
N_HEADS_SB = 4
SB_HEAD_DIM = 128
N_BUCKETS = 32
MAX_DISTANCE = 128
CONV_W = 3
EPS = 1e-6

W_A = N_HEADS_A * HEAD_DIM
W_IDX_Q = N_IDX_HEADS * IDX_DIM
W_SB = N_HEADS_SB * SB_HEAD_DIM

BLK = 128
DSA_UNROLL = 4
SB_Q_BLOCKS = 2
DENOM_ROWS = 16
LOG2E = math.log2(math.e)
SUBLANES = 8
ROW_TILE_MAX = 640
VMEM_LIMIT = 56 * 1024 * 1024
MXU_DTYPE = jnp.bfloat16
INT_MIN = -2 ** 31
SB_UNDERFLOW = -104.0

F32 = jnp.float32


def _cparams(sem):
    return pltpu.CompilerParams(dimension_semantics=sem, vmem_limit_bytes=VMEM_LIMIT)


def _row_tile(n_rows, cap=ROW_TILE_MAX):
    if n_rows <= BLK:
        return n_rows
    best = BLK
    t = BLK
    while t <= min(cap, n_rows):
        if n_rows % t == 0:
            best = t
        t += BLK
    return best


def _rms(x, g):
    return x * lax.rsqrt(jnp.mean(x * x, axis=-1, keepdims=True) + EPS) * g


def _dot(a, b):
    return jnp.dot(a, b, preferred_element_type=F32)


def _split_dot(a, b):
    hi = a.astype(MXU_DTYPE)
    lo = (a - hi.astype(F32)).astype(MXU_DTYPE)
    return _dot(hi, b) + _dot(lo, b)


def _proj_a_kernel(x_ref, g_ref, w_ref, qg_ref, kg_ref, seg_ref,
                   qa_ref, qi_ref, kvk_ref, ka_ref, va_ref, ki_ref):
    xn = _rms(x_ref[...], g_ref[...])
    y = _dot(xn.astype(MXU_DTYPE), w_ref[...])
    q = y[:, :W_A]
    ms = _split_dot(q * q, seg_ref[...])
    qa_ref[...] = (q * lax.rsqrt(ms + EPS) * qg_ref[...]).astype(qa_ref.dtype)
    qi_ref[...] = y[:, W_A:W_A + W_IDX_Q].astype(qi_ref.dtype)
    kvk = y[:, W_A + W_IDX_Q:]
    lane = lax.broadcasted_iota(jnp.int32, kvk.shape, 1)
    is_k = lane < HEAD_DIM
    msk = jnp.sum(jnp.where(is_k, kvk * kvk, 0.0), axis=-1, keepdims=True) * (1.0 / HEAD_DIM)
    kvk = jnp.where(is_k, kvk * lax.rsqrt(msk + EPS) * kg_ref[...], kvk)
    kvk_ref[...] = kvk
    ka_ref[...] = kvk[:, 0:HEAD_DIM].astype(ka_ref.dtype)
    va_ref[...] = kvk[:, HEAD_DIM:2 * HEAD_DIM].astype(va_ref.dtype)
    ki_ref[...] = kvk[:, 2 * HEAD_DIM:2 * HEAD_DIM + IDX_DIM].astype(ki_ref.dtype)


def _proj_b_kernel(x_ref, g_ref, w_ref, qs_ref, ks_ref, vs_ref, ksb_ref, vsb_ref):
    xn = _rms(x_ref[...], g_ref[...])
    y = _dot(xn.astype(MXU_DTYPE), w_ref[...])
    qs_ref[...] = y[:, :W_SB].astype(qs_ref.dtype)
    k = y[:, W_SB:2 * W_SB]
    v = y[:, 2 * W_SB:]
    ks_ref[...] = k
    vs_ref[...] = v
    ksb_ref[...] = k.astype(ksb_ref.dtype)
    vsb_ref[...] = v.astype(vsb_ref.dtype)


def _proj_c_kernel(x_ref, g_ref, w_ref, o_ref):
    xn = _rms(x_ref[...], g_ref[...])
    o_ref[...] = _dot(xn.astype(MXU_DTYPE), w_ref[...])


def _rowwise_call(kernel, x, consts, out_cols_dtypes, name):
    m, d = x.shape
    tm = _row_tile(m)
    in_specs = [pl.BlockSpec((tm, d), lambda i: (i, 0))]
    for c in consts:
        in_specs.append(pl.BlockSpec(c.shape, lambda i, nd=c.ndim: (0,) * nd))
    out_shape = [jax.ShapeDtypeStruct((m, n), dt) for n, dt in out_cols_dtypes]
    out_specs = [pl.BlockSpec((tm, n), lambda i: (i, 0)) for n, _ in out_cols_dtypes]
    return pl.pallas_call(
        kernel, grid=(m // tm,), in_specs=in_specs, out_specs=out_specs, out_shape=out_shape,
        compiler_params=_cparams(("arbitrary",)), name=name)(x, *consts)


def _t5_bucket_np(rel):
    half = N_BUCKETS // 2
    max_exact = half // 2
    n = np.abs(rel)
    large = np.full(n.shape, max_exact, dtype=np.int64)
    steps = half - max_exact
    for t in range(1, steps + 1):
        lhs = n.astype(object) ** steps
        rhs = (max_exact ** steps) * ((MAX_DISTANCE // max_exact) ** t)
        large = large + (np.array(lhs >= rhs, dtype=bool)).astype(np.int64)
    large = np.minimum(large, half - 1)
    return np.where(rel > 0, half, 0) + np.where(n < max_exact, n, large)


def _near_bucket_table():
    m = np.arange(2 * BLK)
    rows = [_t5_bucket_np(d * BLK + BLK - m) for d in (-1, 0, 1)]
    return np.stack(rows).astype(np.int32)


def _pair_transpose(x, n_pairs):
    cols = []
    for p in range(n_pairs):
        t = x[:, p * BLK:(p + 1) * BLK].T
        cols.append(t[:HEAD_DIM])
        cols.append(t[HEAD_DIM:])
    return jnp.concatenate(cols, axis=1)


def _dsa_kernel(qa_ref, qi_ref, kvk_ref, ka_ref, ki_ref, vat_ref, rbt_ref, bidx_ref, o_ref,
                key_scr, bias_scr, m_scr, acc_scr, tie_scr, lg0_scr, lg1_scr, cm0_scr, cm1_scr,
                *, qb0, nsb_total, n_valid, chunk_off, topk):
    i = pl.program_id(1) + qb0
    nsb = jnp.minimum((i + 2 + DSA_UNROLL - 1) // DSA_UNROLL, nsb_total)
    nh = N_HEADS_A
    sbk = DSA_UNROLL * BLK

    qat = (_pair_transpose(qa_ref[...].astype(F32), nh // 2) * (HEAD_DIM ** -0.5 * LOG2E)).astype(MXU_DTYPE)
    qit = _pair_transpose(qi_ref[...].astype(F32), N_IDX_HEADS // 2).astype(MXU_DTYPE)
    kw_t = kvk_ref[...][:, BLK:2 * BLK].T
    w_scale = (IDX_DIM ** -0.5) * (N_IDX_HEADS ** -0.5)
    w_rows = [kw_t[IDX_DIM + h:IDX_DIM + h + 1, :] * w_scale for h in range(N_IDX_HEADS)]

    qpos = i * BLK + lax.broadcasted_iota(jnp.int32, (1, BLK), 1)
    kend = (((qpos + (CHUNK - chunk_off)) >> 6) << 6) + chunk_off
    kend = jnp.minimum(kend, n_valid)
    krow = lax.broadcasted_iota(jnp.int32, (sbk, BLK), 0)

    @pl.when(jnp.logical_and(pl.program_id(0) == 0, pl.program_id(1) == 0))
    def _():
        rbt = rbt_ref[...]
        far_bucket = N_BUCKETS // 2 - 1
        cfar = rbt[:, far_bucket:far_bucket + 1]
        bias_scr[0] = jnp.zeros((BLK, nh * BLK), F32)
        for d in range(3):
            idx = bidx_ref[d:d + 1, :]
            tab = jnp.zeros((nh, 2 * BLK), F32)
            for b in range(N_BUCKETS):
                tab = jnp.where(idx == b, rbt[:, b:b + 1], tab)
            tab = (tab - cfar) * LOG2E
            for h in range(nh):
                trow = jnp.broadcast_to(tab[h:h + 1, :], (BLK, 2 * BLK))
                bias_scr[d + 1, :, h * BLK:(h + 1) * BLK] = pltpu.roll(trow, 0, 1, stride=1, stride_axis=0)[:, BLK:]

    def to_key(v):
        bits = lax.bitcast_convert_type(v, jnp.int32)
        key = bits ^ ((bits >> 31) & 0x7FFFFFFF)
        return jnp.where(v == 0.0, 0, key)

    def from_key(key):
        return lax.bitcast_convert_type(key ^ ((key >> 31) & 0x7FFFFFFF), F32)

    def score_block(s_, carry):
        smin, smax = carry
        off = pl.multiple_of(s_ * sbk, sbk)
        s = jnp.maximum(_dot(ki_ref[pl.ds(off, sbk), :], qit), 0.0)
        sc = s[:, 0:BLK] * w_rows[0]
        for h in range(1, N_IDX_HEADS):
            sc = sc + s[:, h * BLK:(h + 1) * BLK] * w_rows[h]
        adm = (krow + off) < kend
        smin = jnp.minimum(smin, jnp.min(jnp.where(adm, sc, jnp.inf), axis=0, keepdims=True))
        smax = jnp.maximum(smax, jnp.max(jnp.where(adm, sc, -jnp.inf), axis=0, keepdims=True))
        key_scr[pl.ds(off, sbk), :] = jnp.where(adm, to_key(sc), INT_MIN)
        return smin, smax

    smin, smax = lax.fori_loop(0, nsb, score_block,
                               (jnp.full((1, BLK), jnp.inf, F32), jnp.full((1, BLK), -jnp.inf, F32)))

    n_acc = 8 * SUBLANES

    def count(thrs):
        def body(s_, accs):
            off = pl.multiple_of(s_ * sbk, sbk)
            blk = key_scr[pl.ds(off, sbk), :]
            return tuple(acc + jnp.sum(jnp.where(blk >= t, 1.0, 0.0).reshape(sbk // n_acc, n_acc, BLK), axis=0)
                         for acc, t in zip(accs, thrs))
        accs = lax.fori_loop(0, nsb, body, tuple(jnp.zeros((n_acc, BLK), F32) for _ in thrs))
        return [jnp.sum(acc, axis=0, keepdims=True) for acc in accs]

    kf = float(topk)
    n_adm = kend.astype(F32)
    few = n_adm < kf

    def narrow(state, cand, cnt, live):
        lo, c_lo, hi, c_hi = state
        up = cnt >= kf
        take_lo = live & up & (cand > lo)
        take_hi = live & jnp.logical_not(up) & (cand < hi)
        return (jnp.where(take_lo, cand, lo), jnp.where(take_lo, cnt, c_lo),
                jnp.where(take_hi, cand, hi), jnp.where(take_hi, cnt, c_hi))

    def is_open(state):
        lo, c_lo, hi, c_hi = state
        return jnp.logical_not(few) & (c_lo != kf) & (hi - 1 > lo) & (c_lo - c_hi > 2.0)

    state = (to_key(smin), n_adm, to_key(smax) + 1, jnp.zeros((1, BLK), F32))
    zero = jnp.zeros((1, BLK), jnp.int32)
    c_nonneg, c_pos = count([zero, zero + 1])
    all_lanes = jnp.logical_not(few)
    state = narrow(state, zero, c_nonneg, all_lanes)
    state = narrow(state, zero + 1, c_pos, all_lanes)

    group = 2
    max_passes = 4 * 33

    def n_open_of(state):
        return jnp.sum(jnp.where(is_open(state), 1.0, 0.0))

    def search_cond(c):
        t, n_open = c[0], c[1]
        return jnp.logical_and(t < max_passes, n_open > 0.0)

    def search_steps(c):
        t, _, state = c[0], c[1], c[2:]
        for u in range(group):
            lo, _, hi, _ = state
            mid_v = to_key(0.5 * from_key(lo) + 0.5 * from_key(hi - 1))
            mid_k = (lo >> 1) + (hi >> 1) + (lo & hi & 1)
            cand = jnp.where((t + u) % 4 == 3, mid_k, mid_v)
            cand = jnp.minimum(jnp.maximum(cand, lo + 1), hi - 1)
            (cnt,) = count([cand])
            state = narrow(state, cand, cnt, is_open(state))
        return (t + group, n_open_of(state)) + tuple(state)

    res = lax.while_loop(search_cond, search_steps, (jnp.int32(0), n_open_of(state)) + tuple(state))
    lo, c_lo, hi, c_hi = res[2:]
    exact_k = c_lo == kf
    exact_t = jnp.logical_not(exact_k) & jnp.logical_not(hi - 1 > lo)
    pair = jnp.logical_not(few | exact_k | exact_t)

    def largest_below(bound):
        def body(s_, acc):
            off = pl.multiple_of(s_ * sbk, sbk)
            blk = key_scr[pl.ds(off, sbk), :]
            return jnp.maximum(acc, jnp.max(jnp.where(blk < bound, blk, INT_MIN).reshape(sbk // n_acc, n_acc, BLK),
                                            axis=0))
        acc = lax.fori_loop(0, nsb, body, jnp.full((n_acc, BLK), INT_MIN, jnp.int32))
        rows = n_acc
        while rows > SUBLANES:
            rows //= 2
            acc = jnp.maximum(acc[:rows], acc[rows:2 * rows])
        for shift in (4, 2, 1):
            acc = jnp.maximum(acc, pltpu.roll(acc, shift, 0))
        return acc[0:1]

    kth = lax.cond(jnp.sum(jnp.where(pair, 1.0, 0.0)) > 0.0, lambda: largest_below(hi), lambda: lo)
    thr = jnp.where(few, INT_MIN, jnp.where(exact_k, lo - 1, jnp.where(exact_t, lo, kth)))
    n_tie_keep = jnp.where(few | exact_k, 0.0, kf - c_hi)

    m_scr[...] = jnp.full(m_scr.shape, -jnp.inf, F32)
    acc_scr[...] = jnp.zeros(acc_scr.shape, F32)
    tie_scr[...] = jnp.zeros(tie_scr.shape, F32)
    lrow = lax.broadcasted_iota(jnp.int32, (BLK, BLK), 0)
    lcol = lax.broadcasted_iota(jnp.int32, (BLK, BLK), 1)
    lstrict = jnp.where(lcol < lrow, 1.0, 0.0).astype(MXU_DTYPE)

    def stage_a(s_next, lg_ref, cm_ref, with_bias):
        s_ = jnp.minimum(s_next, nsb - 1)
        off = pl.multiple_of(s_ * sbk, sbk)
        masked = jnp.where(s_next < nsb, 0.0, -jnp.inf)
        tie_run = tie_scr[0:1, :]
        selb = []
        for u in range(DSA_UNROLL):
            key = key_scr[pl.ds(off + u * BLK, BLK), :]
            eq = key == thr
            eqf = jnp.where(eq, 1.0, 0.0)
            ties_before = _dot(lstrict, eqf.astype(MXU_DTYPE)) + tie_run
            sel = (key > thr) | (eq & (ties_before < n_tie_keep))
            tie_run = tie_run + jnp.sum(eqf, axis=0, keepdims=True)
            selb.append(jnp.where(sel, masked, -jnp.inf))
        tie_scr[0:1, :] = tie_run
        kab = ka_ref[pl.ds(off, sbk), :]
        tile_idx = [jnp.clip(s_ * DSA_UNROLL + u - i, -2, 1) + 2 for u in range(DSA_UNROLL)]
        for hp in range(nh // 2):
            lg = _dot(kab, qat[:, hp * 2 * BLK:(hp + 1) * 2 * BLK])
            for hh in range(2):
                h = 2 * hp + hh
                cols = slice(h * BLK, (h + 1) * BLK)
                col_max = None
                for u in range(DSA_UNROLL):
                    rows = slice(u * BLK, (u + 1) * BLK)
                    piece = lg[rows, hh * BLK:(hh + 1) * BLK] + selb[u]
                    if with_bias:
                        piece = piece + bias_scr[tile_idx[u], :, cols]
                    lg_ref[rows, cols] = piece
                    part = jnp.max(piece.reshape(BLK // SUBLANES, SUBLANES, BLK), axis=0)
                    col_max = part if col_max is None else jnp.maximum(col_max, part)
                cm_ref[:, cols] = col_max

    def stage_b(s_, lg_ref, cm_ref):
        m_old = m_scr[0:1, :]
        m_new = jnp.maximum(m_old, jnp.max(cm_ref[...], axis=0, keepdims=True))
        m_safe = jnp.where(m_new == -jnp.inf, 0.0, m_new)
        alpha = jnp.exp2(m_old - m_safe)
        p = jnp.exp2(lg_ref[...] - m_safe)
        vat = vat_ref[jnp.minimum(s_, nsb_total - 1)]
        acc_scr[...] = acc_scr[...] * alpha + _dot(vat, p.astype(MXU_DTYPE))
        m_scr[0:1, :] = m_new

    def pair_body(pi, with_bias):
        s_ = 2 * pi
        stage_a(s_ + 1, lg1_scr, cm1_scr, with_bias)
        stage_b(s_, lg0_scr, cm0_scr)
        stage_a(s_ + 2, lg0_scr, cm0_scr, with_bias)
        stage_b(s_ + 1, lg1_scr, cm1_scr)

    def far_pair(pi, carry):
        pair_body(pi, False)
        return carry

    def near_pair(pi, carry):
        pair_body(pi, True)
        return carry

    n_far_sb = jnp.maximum(i - 1, 0) // DSA_UNROLL
    n_far_pairs = jnp.maximum(n_far_sb - 1, 0) // 2
    stage_a(0, lg0_scr, cm0_scr, True)
    lax.fori_loop(0, n_far_pairs, far_pair, 0)
    lax.fori_loop(n_far_pairs, (nsb + 1) // 2, near_pair, 0)

    out_t = acc_scr[0:HEAD_DIM, :] / acc_scr[HEAD_DIM:HEAD_DIM + 1, :]
    for p in range(nh // 2):
        pair = jnp.concatenate([out_t[:, (2 * p) * BLK:(2 * p + 1) * BLK],
                                out_t[:, (2 * p + 1) * BLK:(2 * p + 2) * BLK]], axis=0)
        o_ref[:, p * BLK:(p + 1) * BLK] = pair.T.astype(o_ref.dtype)


def _dsa_call(qa, qi, kvk, ka, ki, va, rbt, bidx, *, qb0, n_valid, chunk_off, topk):
    b, nq, _ = qa.shape
    sbk = DSA_UNROLL * BLK
    pad = (-ka.shape[1]) % sbk
    ka, ki, va = (jnp.pad(a, ((0, 0), (0, pad), (0, 0))) for a in (ka, ki, va))
    nk = ka.shape[1]
    nsb_total = nk // sbk
    vat = va.reshape(b, nsb_total, sbk, HEAD_DIM).transpose(0, 1, 3, 2)
    vat = jnp.concatenate([vat, jnp.ones((b, nsb_total, DENOM_ROWS, sbk), vat.dtype)], axis=2)
    nh = N_HEADS_A
    kernel = functools.partial(_dsa_kernel, qb0=qb0, nsb_total=nsb_total, n_valid=n_valid,
                               chunk_off=chunk_off, topk=topk)
    qspec = lambda w: pl.BlockSpec((None, BLK, w), lambda bi, qi_: (bi, qi_, 0))
    full = lambda a: pl.BlockSpec((None,) + a.shape[1:], lambda bi, qi_, nd=a.ndim: (bi,) + (0,) * (nd - 1))
    const = lambda a: pl.BlockSpec(a.shape, lambda bi, qi_, nd=a.ndim: (0,) * nd)
    return pl.pallas_call(
        kernel, grid=(b, nq // BLK),
        in_specs=[qspec(W_A), qspec(W_IDX_Q), qspec(2 * BLK), full(ka), full(ki), full(vat), const(rbt), const(bidx)],
        out_specs=qspec(W_A),
        out_shape=jax.ShapeDtypeStruct((b, nq, W_A), MXU_DTYPE),
        scratch_shapes=[
            pltpu.VMEM((nk, BLK), jnp.int32),
            pltpu.VMEM((4, BLK, nh * BLK), F32),
            pltpu.VMEM((SUBLANES, nh * BLK), F32),
            pltpu.VMEM((HEAD_DIM + DENOM_ROWS, nh * BLK), F32),
            pltpu.VMEM((SUBLANES, BLK), F32),
            pltpu.VMEM((sbk, nh * BLK), F32),
            pltpu.VMEM((sbk, nh * BLK), F32),
            pltpu.VMEM((SUBLANES, nh * BLK), F32),
            pltpu.VMEM((SUBLANES, nh * BLK), F32),
        ],
        compiler_params=_cparams(("arbitrary", "arbitrary")), name="dsa")(qa, qi, kvk, ka, ki, vat, rbt, bidx)


def _sb_kernel(q_ref, k_ref, v_ref, o_ref, *, qb0, nkb_total):
    bq = q_ref.shape[0]
    nbq = bq // BLK
    i0 = qb0 + pl.program_id(1) * nbq
    j_top = jnp.minimum(i0 + nbq - 1, nkb_total - 1)
    qpos = i0 * BLK + lax.broadcasted_iota(jnp.int32, (bq, BLK), 0)
    kcol = lax.broadcasted_iota(jnp.int32, (bq, BLK), 1)
    trow = lax.broadcasted_iota(jnp.int32, (BLK, BLK), 0)
    tcol = lax.broadcasted_iota(jnp.int32, (BLK, BLK), 1)
    tri = jnp.where(trow > tcol, 1.0, 0.0).astype(MXU_DTYPE)
    scale = SB_HEAD_DIM ** -0.5
    hd = SB_HEAD_DIM
    qs = [q_ref[:, h * hd:(h + 1) * hd] for h in range(N_HEADS_SB)]

    def block(j, later_blocks, acc, masked):
        off = pl.multiple_of(j * BLK, BLK)
        if masked:
            before = (kcol + off) < qpos
        new_lb, new_acc = [], []
        for h in range(N_HEADS_SB):
            kb = k_ref[pl.ds(off, BLK), h * hd:(h + 1) * hd]
            vb = v_ref[pl.ds(off, BLK), h * hd:(h + 1) * hd]
            z = lax.dot_general(qs[h], kb, (((1,), (1,)), ((), ())), preferred_element_type=F32) * scale
            log_keep = -(jnp.maximum(z, 0.0) + jnp.log1p(jnp.exp(-jnp.abs(z))))
            if masked:
                log_keep = jnp.where(before, log_keep, 0.0)
            later = _split_dot(log_keep, tri) + later_blocks[h]
            a = jnp.exp(log_keep + z + later)
            if masked:
                a = jnp.where(before, a, 0.0)
            new_acc.append(acc[h] + _dot(a.astype(MXU_DTYPE), vb))
            new_lb.append(later_blocks[h] + jnp.sum(log_keep, axis=1, keepdims=True))
        return tuple(new_lb), tuple(new_acc)

    def worst(lb):
        return jnp.max(functools.reduce(jnp.maximum, lb))

    later_blocks = tuple(jnp.zeros((bq, 1), F32) for _ in range(N_HEADS_SB))
    acc = tuple(jnp.zeros((bq, hd), F32) for _ in range(N_HEADS_SB))
    for t in range(nbq):
        later_blocks, acc = block(jnp.maximum(j_top - t, 0), later_blocks, acc, True)

    def cond(c):
        j, w, _, _ = c
        return jnp.logical_and(j >= 0, w > SB_UNDERFLOW)

    def body(c):
        j, _, lb, ac = c
        lb, ac = block(j, lb, ac, False)
        return j - 1, worst(lb), lb, ac

    _, _, _, acc = lax.while_loop(cond, body, (j_top - nbq, worst(later_blocks), later_blocks, acc))
    for h in range(N_HEADS_SB):
        o_ref[:, h * hd:(h + 1) * hd] = acc[h].astype(o_ref.dtype)


def _sb_call(q, k, v, *, qb0):
    b, nq, _ = q.shape
    nk = k.shape[1]
    nbq = SB_Q_BLOCKS if nq > BLK else 1
    bq = nbq * BLK
    pad = (-nq) % bq
    if pad:
        q = jnp.pad(q, ((0, 0), (0, pad), (0, 0)))
    kernel = functools.partial(_sb_kernel, qb0=qb0, nkb_total=nk // BLK)
    kv_spec = pl.BlockSpec((None, nk, W_SB), lambda bi, qi_: (bi, 0, 0))
    q_spec = pl.BlockSpec((None, bq, W_SB), lambda bi, qi_: (bi, qi_, 0))
    out = pl.pallas_call(
        kernel, grid=(b, (nq + pad) // bq),
        in_specs=[q_spec, kv_spec, kv_spec], out_specs=q_spec,
        out_shape=jax.ShapeDtypeStruct((b, nq + pad, W_SB), MXU_DTYPE),
        compiler_params=_cparams(("arbitrary", "arbitrary")), name="sb")(q, k, v)
    return out[:, :nq] if pad else out


def _merge_kernel(x_ref, ya_ref, ys_ref, g_ref, wpa_ref, wps_ref, wo_ref, o_ref):
    d = x_ref.shape[-1]
    g = g_ref[...]
    m = (jax.nn.sigmoid(g[:, :d]) * _dot(ya_ref[...], wpa_ref[...])
         + jax.nn.sigmoid(g[:, d:]) * _dot(ys_ref[...], wps_ref[...]))
    o_ref[...] = x_ref[...] + _dot(m.astype(MXU_DTYPE), wo_ref[...])


def _merge_call(x, ya, ys, g, wpa, wps, wo):
    m, d = x.shape
    tm = _row_tile(m)
    row = lambda w: pl.BlockSpec((tm, w), lambda i: (i, 0))
    const = lambda a: pl.BlockSpec(a.shape, lambda i, nd=a.ndim: (0,) * nd)
    return pl.pallas_call(
        _merge_kernel, grid=(m // tm,),
        in_specs=[row(d), row(ya.shape[1]), row(ys.shape[1]), row(g.shape[1]), const(wpa), const(wps), const(wo)],
        out_specs=row(d), out_shape=jax.ShapeDtypeStruct((m, d), F32),
        compiler_params=_cparams(("arbitrary",)), name="merge")(x, ya, ys, g, wpa, wps, wo)


def _ffn_kernel(x_ref, g_ref, st_ref, wa_ref, wu_ref, cw_ref, cb_ref, wd_ref, o_ref,
                h_scr, carry_scr, act_scr):
    t = pl.program_id(1)
    c = pl.program_id(2)
    tm = x_ref.shape[0]

    @pl.when(c == 0)
    def _():
        x = x_ref[...]
        h_scr[...] = _rms(x, g_ref[...]).astype(h_scr.dtype)
        o_ref[...] = x

    @pl.when(t == 0)
    def _():
        carry_scr[c] = st_ref[...]

    h = h_scr[...]
    a = _dot(h, wa_ref[...])
    u = _dot(h, wu_ref[...])
    w0, w1, w2 = cw_ref[0:1, :], cw_ref[1:2, :], cw_ref[2:3, :]
    bias = cb_ref[...]

    def gate(a2, a1, a0, uu):
        cv = bias + (w0 * a2 + w1 * a1 + w2 * a0)
        return cv * jax.nn.sigmoid(cv) * uu

    act_scr[...] = gate(pltpu.roll(a, 2, 0), pltpu.roll(a, 1, 0), a, u)
    prev = carry_scr[c]
    p2 = prev[SUBLANES - 2:SUBLANES - 1, :]
    p1 = prev[SUBLANES - 1:SUBLANES, :]
    top = a[0:SUBLANES]
    rid = lax.broadcasted_iota(jnp.int32, top.shape, 0)
    a1 = jnp.where(rid == 0, p1, pltpu.roll(top, 1, 0))
    a2 = jnp.where(rid == 0, p2, jnp.where(rid == 1, p1, pltpu.roll(top, 2, 0)))
    act_scr[0:SUBLANES, :] = gate(a2, a1, top, u[0:SUBLANES])
    carry_scr[c] = a[tm - SUBLANES:tm]
    o_ref[...] += _dot(act_scr[...].astype(MXU_DTYPE), wd_ref[...])


def _ffn_call(x, g, state, wa, wu, cw, cb, wd, n_chunks):
    b, n, d = x.shape
    dff = wa.shape[1]
    cwid = dff // n_chunks
    tm = _row_tile(n)
    return pl.pallas_call(
        _ffn_kernel, grid=(b, n // tm, n_chunks),
        in_specs=[
            pl.BlockSpec((None, tm, d), lambda bi, t, c: (bi, t, 0)),
            pl.BlockSpec((1, d), lambda bi, t, c: (0, 0)),
            pl.BlockSpec((None, SUBLANES, cwid), lambda bi, t, c: (bi, 0, c)),
            pl.BlockSpec((d, cwid), lambda bi, t, c: (0, c)),
            pl.BlockSpec((d, cwid), lambda bi, t, c: (0, c)),
            pl.BlockSpec((CONV_W, cwid), lambda bi, t, c: (0, c)),
            pl.BlockSpec((1, cwid), lambda bi, t, c: (0, c)),
            pl.BlockSpec((cwid, d), lambda bi, t, c: (c, 0)),
        ],
        out_specs=pl.BlockSpec((None, tm, d), lambda bi, t, c: (bi, t, 0)),
        out_shape=jax.ShapeDtypeStruct((b, n, d), F32),
        scratch_shapes=[
            pltpu.VMEM((tm, d), MXU_DTYPE),
            pltpu.VMEM((n_chunks, SUBLANES, cwid), F32),
            pltpu.VMEM((tm, cwid), F32),
        ],
        compiler_params=_cparams(("arbitrary", "arbitrary", "arbitrary")), name="ffn")(
            x, g, state, wa, wu, cw, cb, wd)


def _ffn_chunks(dff):
    return 2 if dff % (2 * BLK) == 0 else 1


def _layer(x, kv_prefix, conv_state, lw, *, n_new, qb0, n_valid, chunk_off, topk, rbt, bidx):
    b, nq, d = x.shape
    xf = x.reshape(b * nq, d)
    g_mix = lw["ln_mix_g"]
    qa, qi, kvk, ka, va, ki = _rowwise_call(
        _proj_a_kernel, xf, [g_mix, lw["w_a"], lw["qg"], lw["kg"], lw["seg"]],
        [(W_A, MXU_DTYPE), (W_IDX_Q, MXU_DTYPE), (2 * BLK, F32), (HEAD_DIM, MXU_DTYPE), (HEAD_DIM, MXU_DTYPE),
         (IDX_DIM, MXU_DTYPE)], "proj_a")
    qs, ks, vs, ksb, vsb = _rowwise_call(
        _proj_b_kernel, xf, [g_mix, lw["w_b"]],
        [(W_SB, MXU_DTYPE), (W_SB, F32), (W_SB, F32), (W_SB, MXU_DTYPE), (W_SB, MXU_DTYPE)], "proj_b")
    (gates,) = _rowwise_call(_proj_c_kernel, xf, [g_mix, lw["w_c"]], [(2 * d, F32)], "proj_c")

    r3 = lambda a: a.reshape(b, nq, a.shape[-1])
    qa, qi, kvk, ka, va, ki, qs, ks, vs, ksb, vsb = map(r3, (qa, qi, kvk, ka, va, ki, qs, ks, vs, ksb, vsb))

    if kv_prefix is None:
        ka_all, va_all, ki_all, ks_all, vs_all = ka, va, ki, ksb, vsb
    else:
        def join(prefix, new):
            cat = jnp.concatenate([prefix, new[:, :n_new]], axis=1)
            pad = (-cat.shape[1]) % BLK
            return jnp.pad(cat, ((0, 0), (0, pad), (0, 0)))
        ka_all, va_all, ki_all, ks_all, vs_all = (join(p, n_) for p, n_ in zip(kv_prefix, (ka, va, ki, ksb, vsb)))
    ya = _dsa_call(qa, qi, kvk, ka_all, ki_all, va_all, rbt, bidx,
                   qb0=qb0, n_valid=n_valid, chunk_off=chunk_off, topk=topk)
    ys = _sb_call(qs, ks_all, vs_all, qb0=qb0)
    x_mid = _merge_call(xf, ya.reshape(b * nq, W_A), ys.reshape(b * nq, W_SB), gates,
                        lw["w_pa"], lw["w_ps"], lw["w_o"]).reshape(b, nq, d)

    last = x_mid[:, n_new - (CONV_W - 1):n_new].reshape(b * (CONV_W - 1), d)
    (conv_rows,) = _rowwise_call(_proj_c_kernel, last, [lw["ln_ffn_g"], lw["w_up_a"]],
                                 [(lw["w_up_a"].shape[1], F32)], "conv_state")
    x_out = _ffn_call(x_mid, lw["ln_ffn_g"], conv_state, lw["w_up_a"], lw["w_up_u"], lw["conv_w"], lw["conv_b"],
                      lw["w_down"], _ffn_chunks(lw["w_up_a"].shape[1]))
    new_rows = dict(a_k=kvk[:, :n_new, 0:HEAD_DIM], a_v=kvk[:, :n_new, HEAD_DIM:2 * HEAD_DIM],
                    idx_k=kvk[:, :n_new, 2 * HEAD_DIM:2 * HEAD_DIM + IDX_DIM],
                    sb_k=ks[:, :n_new], sb_v=vs[:, :n_new],
                    conv=conv_rows.reshape(b, CONV_W - 1, -1))
    return x_out, new_rows


def kernel(x_prompt, x_sample, cache_a_k, cache_a_v, cache_idx_k, cache_sb_k, cache_sb_v, state_ffn_conv, meta_tokens, rel_bias, ln_mix_g, w_in, q_norm_g, k_norm_g, w_proj_a, w_proj_sb, w_out, ln_ffn_g, w_up, conv_w, conv_b, w_down):
    depth, d_model, _ = w_in.shape
    b_p, seq, _ = x_prompt.shape
    n_meta = meta_tokens.shape[0]
    b_s, n_s, _ = x_sample.shape
    past = cache_a_k.shape[2]
    d_ff = w_down.shape[1]
    n_p = n_meta + seq
    topk_p = min(TOPK_MAX, seq // 4)
    topk_s = min(TOPK_MAX, (past + n_s) // 4)
    assert n_meta <= CHUNK and CONV_W - 1 <= min(n_s, SUBLANES)

    sizes = (W_A, HEAD_DIM, HEAD_DIM, W_IDX_Q, IDX_DIM, N_IDX_HEADS, W_SB, W_SB, W_SB, d_model, d_model)
    offs = np.concatenate([[0], np.cumsum(sizes)])
    col = lambda k: w_in[:, :, offs[k]:offs[k + 1]]
    w_pad = jnp.zeros((depth, d_model, 2 * BLK - 3 * HEAD_DIM - N_IDX_HEADS), w_in.dtype)
    w_a = jnp.concatenate([col(0), col(3), col(1), col(2), col(4), col(5), w_pad], axis=-1).astype(MXU_DTYPE)
    w_b = w_in[:, :, offs[6]:offs[9]].astype(MXU_DTYPE)
    w_c = w_in[:, :, offs[9]:offs[11]].astype(MXU_DTYPE)
    seg_np = np.kron(np.eye(N_HEADS_A), np.full((HEAD_DIM, HEAD_DIM), 1.0 / HEAD_DIM))
    seg = jnp.asarray(seg_np, MXU_DTYPE)
    kg_pad = jnp.concatenate([k_norm_g, jnp.ones((depth, 2 * BLK - HEAD_DIM), k_norm_g.dtype)], axis=-1)
    rbt = rel_bias.T.astype(F32)
    bidx = jnp.asarray(_near_bucket_table())

    layers = []
    for l in range(depth):
        layers.append(dict(
            ln_mix_g=ln_mix_g[l][None], w_a=w_a[l], w_b=w_b[l], w_c=w_c[l],
            qg=jnp.tile(q_norm_g[l], N_HEADS_A)[None], kg=kg_pad[l][None], seg=seg,
            w_pa=w_proj_a[l].astype(MXU_DTYPE), w_ps=w_proj_sb[l].astype(MXU_DTYPE), w_o=w_out[l].astype(MXU_DTYPE),
            ln_ffn_g=ln_ffn_g[l][None], w_up_a=w_up[l][:, :d_ff].astype(MXU_DTYPE),
            w_up_u=w_up[l][:, d_ff:].astype(MXU_DTYPE), conv_w=conv_w[l], conv_b=conv_b[l][None],
            w_down=w_down[l].astype(MXU_DTYPE)))

    np_pad = -(-n_p // BLK) * BLK
    meta = jnp.broadcast_to(meta_tokens.astype(x_prompt.dtype)[None], (b_p, n_meta, d_model))
    xp = jnp.concatenate([meta, x_prompt, jnp.zeros((b_p, np_pad - n_p, d_model), x_prompt.dtype)], axis=1)
    zero_state = jnp.zeros((b_p, SUBLANES, d_ff), F32)

    ns_pad = -(-n_s // BLK) * BLK
    assert past % BLK == 0 and ns_pad == BLK
    xs = jnp.pad(x_sample, ((0, 0), (0, ns_pad - n_s), (0, 0)))

    outs_p, outs_s = [], []
    for l in range(depth):
        lw = layers[l]
        xp, rows_p = _layer(xp, None, zero_state, lw, n_new=n_p, qb0=0, n_valid=n_p, chunk_off=n_meta,
                            topk=topk_p, rbt=rbt, bidx=bidx)
        outs_p.append(rows_p)

        prefix = (cache_a_k[l].reshape(b_s, past, HEAD_DIM).astype(MXU_DTYPE),
                  cache_a_v[l].reshape(b_s, past, HEAD_DIM).astype(MXU_DTYPE),
                  cache_idx_k[l].astype(MXU_DTYPE),
                  cache_sb_k[l].reshape(b_s, past, W_SB).astype(MXU_DTYPE),
                  cache_sb_v[l].reshape(b_s, past, W_SB).astype(MXU_DTYPE))
        st = jnp.pad(state_ffn_conv[l].astype(F32), ((0, 0), (SUBLANES - (CONV_W - 1), 0), (0, 0)))
        xs, rows_s = _layer(xs, prefix, st, lw, n_new=n_s, qb0=past // BLK, n_valid=past + n_s, chunk_off=0,
                            topk=topk_s, rbt=rbt, bidx=bidx)
        outs_s.append(rows_s)

    def stack(outs, name, shape_tail):
        a = jnp.stack([o[name] for o in outs])
        return a.reshape(a.shape[:3] + shape_tail)

    def group(outs):
        return (stack(outs, "a_k", (1, HEAD_DIM)), stack(outs, "a_v", (1, HEAD_DIM)), stack(outs, "idx_k", (IDX_DIM,)),
                stack(outs, "sb_k", (N_HEADS_SB, SB_HEAD_DIM)), stack(outs, "sb_v", (N_HEADS_SB, SB_HEAD_DIM)),
                stack(outs, "conv", (d_ff,)))

    y_prompt = xp[:, n_meta:n_p]
    y_sample = xs[:, :n_s]
    return (y_prompt, y_sample) + group(outs_p) + group(outs_s)
```

```python
import functools
import math

import numpy as np
import jax
import jax.numpy as jnp
from jax import lax
from jax.experimental import pallas as pl
from jax.experimental.pallas import tpu as pltpu

CHUNK = 64
HEAD_DIM = 64
N_HEADS_A = 8
N_IDX_HEADS = 4
IDX_DIM = 64
TOPK_MAX = 256
N_HEADS_SB = 4
SB_HEAD_DIM = 128
N_BUCKETS = 32
MAX_DISTANCE = 128
CONV_W = 3
EPS = 1e-6

W_A = N_HEADS_A * HEAD_DIM
W_IDX_Q = N_IDX_HEADS * IDX_DIM
W_SB = N_HEADS_SB * SB_HEAD_DIM

BLK = 128
DSA_UNROLL = 4
SCORE_UNROLL = 2 * DSA_UNROLL
SB_Q_BLOCKS = 2
DENOM_ROWS = 16
LOG2E = math.log2(math.e)
SUBLANES = 8
ROW_TILE_MAX = 640
VMEM_LIMIT = 56 * 1024 * 1024
MXU_DTYPE = jnp.bfloat16
INT_MIN = -2 ** 31
SB_UNDERFLOW = -104.0

F32 = jnp.float32


def _cparams(sem):
    return pltpu.CompilerParams(dimension_semantics=sem, vmem_limit_bytes=VMEM_LIMIT)


def _row_tile(n_rows, cap=ROW_TILE_MAX):
    if n_rows <= BLK:
        return n_rows
    best = BLK
    t = BLK
    while t <= min(cap, n_rows):
        if n_rows % t == 0:
            best = t
        t += BLK
    return best


def _rms(x, g):
    return x * lax.rsqrt(jnp.mean(x * x, axis=-1, keepdims=True) + EPS) * g


def _dot(a, b):
    return jnp.dot(a, b, preferred_element_type=F32)


def _split_dot(a, b):
    hi = a.astype(MXU_DTYPE)
    lo = (a - hi.astype(F32)).astype(MXU_DTYPE)
    return _dot(hi, b) + _dot(lo, b)


def _proj_a_kernel(x_ref, g_ref, w_ref, qg_ref, kg_ref, seg_ref,
                   qa_ref, qi_ref, kvk_ref, ka_ref, va_ref, ki_ref):
    xn = _rms(x_ref[...], g_ref[...])
    y = _dot(xn.astype(MXU_DTYPE), w_ref[...])
    q = y[:, :W_A]
    ms = _split_dot(q * q, seg_ref[...])
    qa_ref[...] = (q * lax.rsqrt(ms + EPS) * qg_ref[...]).astype(qa_ref.dtype)
    qi_ref[...] = y[:, W_A:W_A + W_IDX_Q].astype(qi_ref.dtype)
    kvk = y[:, W_A + W_IDX_Q:]
    lane = lax.broadcasted_iota(jnp.int32, kvk.shape, 1)
    is_k = lane < HEAD_DIM
    msk = jnp.sum(jnp.where(is_k, kvk * kvk, 0.0), axis=-1, keepdims=True) * (1.0 / HEAD_DIM)
    kvk = jnp.where(is_k, kvk * lax.rsqrt(msk + EPS) * kg_ref[...], kvk)
    kvk_ref[...] = kvk
    ka_ref[...] = kvk[:, 0:HEAD_DIM].astype(ka_ref.dtype)
    va_ref[...] = kvk[:, HEAD_DIM:2 * HEAD_DIM].astype(va_ref.dtype)
    ki_ref[...] = kvk[:, 2 * HEAD_DIM:2 * HEAD_DIM + IDX_DIM].astype(ki_ref.dtype)


def _proj_b_kernel(x_ref, g_ref, w_ref, qs_ref, ks_ref, vs_ref, ksb_ref, vsb_ref):
    xn = _rms(x_ref[...], g_ref[...])
    y = _dot(xn.astype(MXU_DTYPE), w_ref[...])
    qs_ref[...] = y[:, :W_SB].astype(qs_ref.dtype)
    k = y[:, W_SB:2 * W_SB]
    v = y[:, 2 * W_SB:]
    ks_ref[...] = k
    vs_ref[...] = v
    ksb_ref[...] = k.astype(ksb_ref.dtype)
    vsb_ref[...] = v.astype(vsb_ref.dtype)


def _proj_c_kernel(x_ref, g_ref, w_ref, o_ref):
    xn = _rms(x_ref[...], g_ref[...])
    o_ref[...] = _dot(xn.astype(MXU_DTYPE), w_ref[...])


def _rowwise_call(kernel, x, consts, out_cols_dtypes, name):
    m, d = x.shape
    tm = _row_tile(m)
    in_specs = [pl.BlockSpec((tm, d), lambda i: (i, 0))]
    for c in consts:
        in_specs.append(pl.BlockSpec(c.shape, lambda i, nd=c.ndim: (0,) * nd))
    out_shape = [jax.ShapeDtypeStruct((m, n), dt) for n, dt in out_cols_dtypes]
    out_specs = [pl.BlockSpec((tm, n), lambda i: (i, 0)) for n, _ in out_cols_dtypes]
    return pl.pallas_call(
        kernel, grid=(m // tm,), in_specs=in_specs, out_specs=out_specs, out_shape=out_shape,
        compiler_params=_cparams(("arbitrary",)), name=name)(x, *consts)


def _t5_bucket_np(rel):
    half = N_BUCKETS // 2
    max_exact = half // 2
    n = np.abs(rel)
    large = np.full(n.shape, max_exact, dtype=np.int64)
    steps = half - max_exact
    for t in range(1, steps + 1):
        lhs = n.astype(object) ** steps
        rhs = (max_exact ** steps) * ((MAX_DISTANCE // max_exact) ** t)
        large = large + (np.array(lhs >= rhs, dtype=bool)).astype(np.int64)
    large = np.minimum(large, half - 1)
    return np.where(rel > 0, half, 0) + np.where(n < max_exact, n, large)


def _near_bucket_table():
    m = np.arange(2 * BLK)
    rows = [_t5_bucket_np(d * BLK + BLK - m) for d in (-1, 0, 1)]
    return np.stack(rows).astype(np.int32)


def _pair_transpose(x, n_pairs):
    cols = []
    for p in range(n_pairs):
        t = x[:, p * BLK:(p + 1) * BLK].T
        cols.append(t[:HEAD_DIM])
        cols.append(t[HEAD_DIM:])
    return jnp.concatenate(cols, axis=1)


def _dsa_kernel(qa_ref, qi_ref, kvk_ref, ka_ref, ki_ref, vat_ref, rbt_ref, bidx_ref, o_ref,
                key_scr, bias_scr, m_scr, acc_scr, tie_scr, lg0_scr, lg1_scr, cm0_scr, cm1_scr,
                *, qb0, nsb_total, n_valid, chunk_off, topk):
    i = pl.program_id(1) + qb0
    nsb = jnp.minimum((i + 2 + DSA_UNROLL - 1) // DSA_UNROLL, nsb_total)
    nh = N_HEADS_A
    sbk = DSA_UNROLL * BLK

    qat = (_pair_transpose(qa_ref[...].astype(F32), nh // 2) * (HEAD_DIM ** -0.5 * LOG2E)).astype(MXU_DTYPE)
    qit = _pair_transpose(qi_ref[...].astype(F32), N_IDX_HEADS // 2).astype(MXU_DTYPE)
    kw_t = kvk_ref[...][:, BLK:2 * BLK].T
    w_scale = (IDX_DIM ** -0.5) * (N_IDX_HEADS ** -0.5)
    w_rows = [kw_t[IDX_DIM + h:IDX_DIM + h + 1, :] * w_scale for h in range(N_IDX_HEADS)]

    qpos = i * BLK + lax.broadcasted_iota(jnp.int32, (1, BLK), 1)
    kend = (((qpos + (CHUNK - chunk_off)) >> 6) << 6) + chunk_off
    kend = jnp.minimum(kend, n_valid)
    krow = lax.broadcasted_iota(jnp.int32, (sbk, BLK), 0)

    @pl.when(jnp.logical_and(pl.program_id(0) == 0, pl.program_id(1) == 0))
    def _():
        rbt = rbt_ref[...]
        far_bucket = N_BUCKETS // 2 - 1
        cfar = rbt[:, far_bucket:far_bucket + 1]
        bias_scr[0] = jnp.zeros((BLK, nh * BLK), F32)
        for d in range(3):
            idx = bidx_ref[d:d + 1, :]
            tab = jnp.zeros((nh, 2 * BLK), F32)
            for b in range(N_BUCKETS):
                tab = jnp.where(idx == b, rbt[:, b:b + 1], tab)
            tab = (tab - cfar) * LOG2E
            for h in range(nh):
                trow = jnp.broadcast_to(tab[h:h + 1, :], (BLK, 2 * BLK))
                bias_scr[d + 1, :, h * BLK:(h + 1) * BLK] = pltpu.roll(trow, 0, 1, stride=1, stride_axis=0)[:, BLK:]

    def to_key(v):
        bits = lax.bitcast_convert_type(v, jnp.int32)
        key = bits ^ ((bits >> 31) & 0x7FFFFFFF)
        return jnp.where(v == 0.0, 0, key)

    def from_key(key):
        return lax.bitcast_convert_type(key ^ ((key >> 31) & 0x7FFFFFFF), F32)

    sck = SCORE_UNROLL * BLK
    srow = lax.broadcasted_iota(jnp.int32, (sck, BLK), 0)

    def score_block(s_, carry):
        smin, smax = carry
        off = pl.multiple_of(s_ * sck, sck)
        s = jnp.maximum(_dot(ki_ref[pl.ds(off, sck), :], qit), 0.0)
        sc = s[:, 0:BLK] * w_rows[0]
        for h in range(1, N_IDX_HEADS):
            sc = sc + s[:, h * BLK:(h + 1) * BLK] * w_rows[h]
        adm = (srow + off) < kend
        smin = jnp.minimum(smin, jnp.min(jnp.where(adm, sc, jnp.inf), axis=0, keepdims=True))
        smax = jnp.maximum(smax, jnp.max(jnp.where(adm, sc, -jnp.inf), axis=0, keepdims=True))
        key_scr[pl.ds(off, sck), :] = jnp.where(adm, to_key(sc), INT_MIN)
        return smin, smax

    n_score = (nsb * sbk + sck - 1) // sck
    smin, smax = lax.fori_loop(0, n_score, score_block,
                               (jnp.full((1, BLK), jnp.inf, F32), jnp.full((1, BLK), -jnp.inf, F32)))

    n_acc = 8 * SUBLANES

    def count(thrs):
        def body(s_, accs):
            off = pl.multiple_of(s_ * sbk, sbk)
            blk = key_scr[pl.ds(off, sbk), :]
            return tuple(acc + jnp.sum(jnp.where(blk >= t, 1.0, 0.0).reshape(sbk // n_acc, n_acc, BLK), axis=0)
                         for acc, t in zip(accs, thrs))
        accs = lax.fori_loop(0, nsb, body, tuple(jnp.zeros((n_acc, BLK), F32) for _ in thrs))
        return [jnp.sum(acc, axis=0, keepdims=True) for acc in accs]

    kf = float(topk)
    n_adm = kend.astype(F32)
    few = n_adm < kf

    def narrow(state, cand, cnt, live):
        lo, c_lo, hi, c_hi = state
        up = cnt >= kf
        take_lo = live & up & (cand > lo)
        take_hi = live & jnp.logical_not(up) & (cand < hi)
        return (jnp.where(take_lo, cand, lo), jnp.where(take_lo, cnt, c_lo),
                jnp.where(take_hi, cand, hi), jnp.where(take_hi, cnt, c_hi))

    def is_open(state):
        lo, c_lo, hi, c_hi = state
        return jnp.logical_not(few) & (c_lo != kf) & (hi - 1 > lo) & (c_lo - c_hi > 2.0)

    state = (to_key(smin), n_adm, to_key(smax) + 1, jnp.zeros((1, BLK), F32))
    zero = jnp.zeros((1, BLK), jnp.int32)
    c_nonneg, c_pos = count([zero, zero + 1])
    all_lanes = jnp.logical_not(few)
    state = narrow(state, zero, c_nonneg, all_lanes)
    state = narrow(state, zero + 1, c_pos, all_lanes)

    group = 2
    max_passes = 4 * 33

    def n_open_of(state):
        return jnp.sum(jnp.where(is_open(state), 1.0, 0.0))

    def search_cond(c):
        t, n_open = c[0], c[1]
        return jnp.logical_and(t < max_passes, n_open > 0.0)

    def search_steps(c):
        t, _, state = c[0], c[1], c[2:]
        for u in range(group):
            lo, _, hi, _ = state
            mid_v = to_key(0.5 * from_key(lo) + 0.5 * from_key(hi - 1))
            mid_k = (lo >> 1) + (hi >> 1) + (lo & hi & 1)
            cand = jnp.where((t + u) % 4 == 3, mid_k, mid_v)
            cand = jnp.minimum(jnp.maximum(cand, lo + 1), hi - 1)
            (cnt,) = count([cand])
            state = narrow(state, cand, cnt, is_open(state))
        return (t + group, n_open_of(state)) + tuple(state)

    res = lax.while_loop(search_cond, search_steps, (jnp.int32(0), n_open_of(state)) + tuple(state))
    lo, c_lo, hi, c_hi = res[2:]
    exact_k = c_lo == kf
    exact_t = jnp.logical_not(exact_k) & jnp.logical_not(hi - 1 > lo)
    pair = jnp.logical_not(few | exact_k | exact_t)

    def largest_below(bound):
        def body(s_, acc):
            off = pl.multiple_of(s_ * sbk, sbk)
            blk = key_scr[pl.ds(off, sbk), :]
            return jnp.maximum(acc, jnp.max(jnp.where(blk < bound, blk, INT_MIN).reshape(sbk // n_acc, n_acc, BLK),
                                            axis=0))
        acc = lax.fori_loop(0, nsb, body, jnp.full((n_acc, BLK), INT_MIN, jnp.int32))
        rows = n_acc
        while rows > SUBLANES:
            rows //= 2
            acc = jnp.maximum(acc[:rows], acc[rows:2 * rows])
        for shift in (4, 2, 1):
            acc = jnp.maximum(acc, pltpu.roll(acc, shift, 0))
        return acc[0:1]

    kth = lax.cond(jnp.sum(jnp.where(pair, 1.0, 0.0)) > 0.0, lambda: largest_below(hi), lambda: lo)
    thr = jnp.where(few, INT_MIN, jnp.where(exact_k, lo - 1, jnp.where(exact_t, lo, kth)))
    n_tie_keep = jnp.where(few | exact_k, 0.0, kf - c_hi)

    m_scr[...] = jnp.full(m_scr.shape, -jnp.inf, F32)
    acc_scr[...] = jnp.zeros(acc_scr.shape, F32)
    tie_scr[...] = jnp.zeros(tie_scr.shape, F32)
    lrow = lax.broadcasted_iota(jnp.int32, (BLK, BLK), 0)
    lcol = lax.broadcasted_iota(jnp.int32, (BLK, BLK), 1)
    lstrict = jnp.where(lcol < lrow, 1.0, 0.0).astype(MXU_DTYPE)

    def stage_a(s_next, lg_ref, cm_ref, with_bias):
        s_ = jnp.minimum(s_next, nsb - 1)
        off = pl.multiple_of(s_ * sbk, sbk)
        masked = jnp.where(s_next < nsb, 0.0, -jnp.inf)
        tie_run = tie_scr[0:1, :]
        selb = []
        for u in range(DSA_UNROLL):
            key = key_scr[pl.ds(off + u * BLK, BLK), :]
            eq = key == thr
            eqf = jnp.where(eq, 1.0, 0.0)
            ties_before = _dot(lstrict, eqf.astype(MXU_DTYPE)) + tie_run
            sel = (key > thr) | (eq & (ties_before < n_tie_keep))
            tie_run = tie_run + jnp.sum(eqf, axis=0, keepdims=True)
            selb.append(jnp.where(sel, masked, -jnp.inf))
        tie_scr[0:1, :] = tie_run
        kab = ka_ref[pl.ds(off, sbk), :]
        tile_idx = [jnp.clip(s_ * DSA_UNROLL + u - i, -2, 1) + 2 for u in range(DSA_UNROLL)]
        for hp in range(nh // 2):
            lg = _dot(kab, qat[:, hp * 2 * BLK:(hp + 1) * 2 * BLK])
            for hh in range(2):
                h = 2 * hp + hh
                cols = slice(h * BLK, (h + 1) * BLK)
                col_max = None
                for u in range(DSA_UNROLL):
                    rows = slice(u * BLK, (u + 1) * BLK)
                    piece = lg[rows, hh * BLK:(hh + 1) * BLK] + selb[u]
                    if with_bias:
                        piece = piece + bias_scr[tile_idx[u], :, cols]
                    lg_ref[rows, cols] = piece
                    part = jnp.max(piece.reshape(BLK // SUBLANES, SUBLANES, BLK), axis=0)
                    col_max = part if col_max is None else jnp.maximum(col_max, part)
                cm_ref[:, cols] = col_max

    def stage_b(s_, lg_ref, cm_ref):
        m_old = m_scr[0:1, :]
        m_new = jnp.maximum(m_old, jnp.max(cm_ref[...], axis=0, keepdims=True))
        m_safe = jnp.where(m_new == -jnp.inf, 0.0, m_new)
        alpha = jnp.exp2(m_old - m_safe)
        p = jnp.exp2(lg_ref[...] - m_safe)
        vat = vat_ref[jnp.minimum(s_, nsb_total - 1)]
        acc_scr[...] = acc_scr[...] * alpha + _dot(vat, p.astype(MXU_DTYPE))
        m_scr[0:1, :] = m_new

    def pair_body(pi, with_bias):
        s_ = 2 * pi
        stage_a(s_ + 1, lg1_scr, cm1_scr, with_bias)
        stage_b(s_, lg0_scr, cm0_scr)
        stage_a(s_ + 2, lg0_scr, cm0_scr, with_bias)
        stage_b(s_ + 1, lg1_scr, cm1_scr)

    def far_pair(pi, carry):
        pair_body(pi, False)
        return carry

    def near_pair(pi, carry):
        pair_body(pi, True)
        return carry

    n_far_sb = jnp.maximum(i - 1, 0) // DSA_UNROLL
    n_far_pairs = jnp.maximum(n_far_sb - 1, 0) // 2
    stage_a(0, lg0_scr, cm0_scr, True)
    lax.fori_loop(0, n_far_pairs, far_pair, 0)
    lax.fori_loop(n_far_pairs, nsb // 2, near_pair, 0)

    @pl.when(nsb % 2 == 1)
    def _():
        stage_b(nsb - 1, lg0_scr, cm0_scr)

    out_t = acc_scr[0:HEAD_DIM, :] / acc_scr[HEAD_DIM:HEAD_DIM + 1, :]
    for p in range(nh // 2):
        pair = jnp.concatenate([out_t[:, (2 * p) * BLK:(2 * p + 1) * BLK],
                                out_t[:, (2 * p + 1) * BLK:(2 * p + 2) * BLK]], axis=0)
        o_ref[:, p * BLK:(p + 1) * BLK] = pair.T.astype(o_ref.dtype)


def _dsa_call(qa, qi, kvk, ka, ki, va, rbt, bidx, *, qb0, n_valid, chunk_off, topk):
    b, nq, _ = qa.shape
    sbk = DSA_UNROLL * BLK
    pad = (-ka.shape[1]) % (SCORE_UNROLL * BLK)
    ka, ki, va = (jnp.pad(a, ((0, 0), (0, pad), (0, 0))) for a in (ka, ki, va))
    nk = ka.shape[1]
    nsb_total = nk // sbk
    vat = va.reshape(b, nsb_total, sbk, HEAD_DIM).transpose(0, 1, 3, 2)
    vat = jnp.concatenate([vat, jnp.ones((b, nsb_total, DENOM_ROWS, sbk), vat.dtype)], axis=2)
    nh = N_HEADS_A
    kernel = functools.partial(_dsa_kernel, qb0=qb0, nsb_total=nsb_total, n_valid=n_valid,
                               chunk_off=chunk_off, topk=topk)
    qspec = lambda w: pl.BlockSpec((None, BLK, w), lambda bi, qi_: (bi, qi_, 0))
    full = lambda a: pl.BlockSpec((None,) + a.shape[1:], lambda bi, qi_, nd=a.ndim: (bi,) + (0,) * (nd - 1))
    const = lambda a: pl.BlockSpec(a.shape, lambda bi, qi_, nd=a.ndim: (0,) * nd)
    return pl.pallas_call(
        kernel, grid=(b, nq // BLK),
        in_specs=[qspec(W_A), qspec(W_IDX_Q), qspec(2 * BLK), full(ka), full(ki), full(vat), const(rbt), const(bidx)],
        out_specs=qspec(W_A),
        out_shape=jax.ShapeDtypeStruct((b, nq, W_A), MXU_DTYPE),
        scratch_shapes=[
            pltpu.VMEM((nk, BLK), jnp.int32),
            pltpu.VMEM((4, BLK, nh * BLK), F32),
            pltpu.VMEM((SUBLANES, nh * BLK), F32),
            pltpu.VMEM((HEAD_DIM + DENOM_ROWS, nh * BLK), F32),
            pltpu.VMEM((SUBLANES, BLK), F32),
            pltpu.VMEM((sbk, nh * BLK), F32),
            pltpu.VMEM((sbk, nh * BLK), F32),
            pltpu.VMEM((SUBLANES, nh * BLK), F32),
            pltpu.VMEM((SUBLANES, nh * BLK), F32),
        ],
        compiler_params=_cparams(("arbitrary", "arbitrary")), name="dsa")(qa, qi, kvk, ka, ki, vat, rbt, bidx)


def _sb_kernel(q_ref, k_ref, v_ref, o_ref, *, qb0, nkb_total):
    bq = q_ref.shape[0]
    nbq = bq // BLK
    i0 = qb0 + pl.program_id(1) * nbq
    j_top = jnp.minimum(i0 + nbq - 1, nkb_total - 1)
    qpos = i0 * BLK + lax.broadcasted_iota(jnp.int32, (bq, BLK), 0)
    kcol = lax.broadcasted_iota(jnp.int32, (bq, BLK), 1)
    trow = lax.broadcasted_iota(jnp.int32, (BLK, BLK), 0)
    tcol = lax.broadcasted_iota(jnp.int32, (BLK, BLK), 1)
    tri = jnp.where(trow > tcol, 1.0, 0.0).astype(MXU_DTYPE)
    scale = SB_HEAD_DIM ** -0.5
    hd = SB_HEAD_DIM
    qs = [q_ref[:, h * hd:(h + 1) * hd] for h in range(N_HEADS_SB)]

    def block(j, later_blocks, acc, masked):
        off = pl.multiple_of(j * BLK, BLK)
        if masked:
            before = (kcol + off) < qpos
        new_lb, new_acc = [], []
        for h in range(N_HEADS_SB):
            kb = k_ref[pl.ds(off, BLK), h * hd:(h + 1) * hd]
            vb = v_ref[pl.ds(off, BLK), h * hd:(h + 1) * hd]
            z = lax.dot_general(qs[h], kb, (((1,), (1,)), ((), ())), preferred_element_type=F32) * scale
            log_keep = -(jnp.maximum(z, 0.0) + jnp.log1p(jnp.exp(-jnp.abs(z))))
            if masked:
                log_keep = jnp.where(before, log_keep, 0.0)
            later = _split_dot(log_keep, tri) + later_blocks[h]
            a = jnp.exp(log_keep + z + later)
            if masked:
                a = jnp.where(before, a, 0.0)
            new_acc.append(acc[h] + _dot(a.astype(MXU_DTYPE), vb))
            new_lb.append(later_blocks[h] + jnp.sum(log_keep, axis=1, keepdims=True))
        return tuple(new_lb), tuple(new_acc)

    def worst(lb):
        return jnp.max(functools.reduce(jnp.maximum, lb))

    later_blocks = tuple(jnp.zeros((bq, 1), F32) for _ in range(N_HEADS_SB))
    acc = tuple(jnp.zeros((bq, hd), F32) for _ in range(N_HEADS_SB))
    for t in range(nbq):
        later_blocks, acc = block(jnp.maximum(j_top - t, 0), later_blocks, acc, True)

    def cond(c):
        j, w, _, _ = c
        return jnp.logical_and(j >= 0, w > SB_UNDERFLOW)

    def body(c):
        j, _, lb, ac = c
        lb, ac = block(j, lb, ac, False)
        return j - 1, worst(lb), lb, ac

    _, _, _, acc = lax.while_loop(cond, body, (j_top - nbq, worst(later_blocks), later_blocks, acc))
    for h in range(N_HEADS_SB):
        o_ref[:, h * hd:(h + 1) * hd] = acc[h].astype(o_ref.dtype)


def _sb_call(q, k, v, *, qb0):
    b, nq, _ = q.shape
    nk = k.shape[1]
    nbq = SB_Q_BLOCKS if nq > BLK else 1
    bq = nbq * BLK
    pad = (-nq) % bq
    if pad:
        q = jnp.pad(q, ((0, 0), (0, pad), (0, 0)))
    kernel = functools.partial(_sb_kernel, qb0=qb0, nkb_total=nk // BLK)
    kv_spec = pl.BlockSpec((None, nk, W_SB), lambda bi, qi_: (bi, 0, 0))
    q_spec = pl.BlockSpec((None, bq, W_SB), lambda bi, qi_: (bi, qi_, 0))
    out = pl.pallas_call(
        kernel, grid=(b, (nq + pad) // bq),
        in_specs=[q_spec, kv_spec, kv_spec], out_specs=q_spec,
        out_shape=jax.ShapeDtypeStruct((b, nq + pad, W_SB), MXU_DTYPE),
        compiler_params=_cparams(("arbitrary", "arbitrary")), name="sb")(q, k, v)
    return out[:, :nq] if pad else out


def _merge_kernel(x_ref, ya_ref, ys_ref, g_ref, wpa_ref, wps_ref, wo_ref, o_ref):
    d = x_ref.shape[-1]
    g = g_ref[...]
    m = (jax.nn.sigmoid(g[:, :d]) * _dot(ya_ref[...], wpa_ref[...])
         + jax.nn.sigmoid(g[:, d:]) * _dot(ys_ref[...], wps_ref[...]))
    o_ref[...] = x_ref[...] + _dot(m.astype(MXU_DTYPE), wo_ref[...])


def _merge_call(x, ya, ys, g, wpa, wps, wo):
    m, d = x.shape
    tm = _row_tile(m)
    row = lambda w: pl.BlockSpec((tm, w), lambda i: (i, 0))
    const = lambda a: pl.BlockSpec(a.shape, lambda i, nd=a.ndim: (0,) * nd)
    return pl.pallas_call(
        _merge_kernel, grid=(m // tm,),
        in_specs=[row(d), row(ya.shape[1]), row(ys.shape[1]), row(g.shape[1]), const(wpa), const(wps), const(wo)],
        out_specs=row(d), out_shape=jax.ShapeDtypeStruct((m, d), F32),
        compiler_params=_cparams(("arbitrary",)), name="merge")(x, ya, ys, g, wpa, wps, wo)


def _ffn_kernel(x_ref, g_ref, st_ref, wa_ref, wu_ref, cw_ref, cb_ref, wd_ref, o_ref,
                h_scr, carry_scr, act_scr):
    t = pl.program_id(1)
    c = pl.program_id(2)
    tm = x_ref.shape[0]

    @pl.when(c == 0)
    def _():
        x = x_ref[...]
        h_scr[...] = _rms(x, g_ref[...]).astype(h_scr.dtype)
        o_ref[...] = x

    @pl.when(t == 0)
    def _():
        carry_scr[c] = st_ref[...]

    h = h_scr[...]
    a = _dot(h, wa_ref[...])
    u = _dot(h, wu_ref[...])
    w0, w1, w2 = cw_ref[0:1, :], cw_ref[1:2, :], cw_ref[2:3, :]
    bias = cb_ref[...]

    def gate(a2, a1, a0, uu):
        cv = bias + (w0 * a2 + w1 * a1 + w2 * a0)
        return cv * jax.nn.sigmoid(cv) * uu

    act_scr[...] = gate(pltpu.roll(a, 2, 0), pltpu.roll(a, 1, 0), a, u)
    prev = carry_scr[c]
    p2 = prev[SUBLANES - 2:SUBLANES - 1, :]
    p1 = prev[SUBLANES - 1:SUBLANES, :]
    top = a[0:SUBLANES]
    rid = lax.broadcasted_iota(jnp.int32, top.shape, 0)
    a1 = jnp.where(rid == 0, p1, pltpu.roll(top, 1, 0))
    a2 = jnp.where(rid == 0, p2, jnp.where(rid == 1, p1, pltpu.roll(top, 2, 0)))
    act_scr[0:SUBLANES, :] = gate(a2, a1, top, u[0:SUBLANES])
    carry_scr[c] = a[tm - SUBLANES:tm]
    o_ref[...] += _dot(act_scr[...].astype(MXU_DTYPE), wd_ref[...])


def _ffn_call(x, g, state, wa, wu, cw, cb, wd, n_chunks):
    b, n, d = x.shape
    dff = wa.shape[1]
    cwid = dff // n_chunks
    tm = _row_tile(n)
    return pl.pallas_call(
        _ffn_kernel, grid=(b, n // tm, n_chunks),
        in_specs=[
            pl.BlockSpec((None, tm, d), lambda bi, t, c: (bi, t, 0)),
            pl.BlockSpec((1, d), lambda bi, t, c: (0, 0)),
            pl.BlockSpec((None, SUBLANES, cwid), lambda bi, t, c: (bi, 0, c)),
            pl.BlockSpec((d, cwid), lambda bi, t, c: (0, c)),
            pl.BlockSpec((d, cwid), lambda bi, t, c: (0, c)),
            pl.BlockSpec((CONV_W, cwid), lambda bi, t, c: (0, c)),
            pl.BlockSpec((1, cwid), lambda bi, t, c: (0, c)),
            pl.BlockSpec((cwid, d), lambda bi, t, c: (c, 0)),
        ],
        out_specs=pl.BlockSpec((None, tm, d), lambda bi, t, c: (bi, t, 0)),
        out_shape=jax.ShapeDtypeStruct((b, n, d), F32),
        scratch_shapes=[
            pltpu.VMEM((tm, d), MXU_DTYPE),
            pltpu.VMEM((n_chunks, SUBLANES, cwid), F32),
            pltpu.VMEM((tm, cwid), F32),
        ],
        compiler_params=_cparams(("arbitrary", "arbitrary", "arbitrary")), name="ffn")(
            x, g, state, wa, wu, cw, cb, wd)


def _ffn_chunks(dff):
    return 2 if dff % (2 * BLK) == 0 else 1


def _layer(x, kv_prefix, conv_state, lw, *, n_new, qb0, n_valid, chunk_off, topk, rbt, bidx):
    b, nq, d = x.shape
    xf = x.reshape(b * nq, d)
    g_mix = lw["ln_mix_g"]
    qa, qi, kvk, ka, va, ki = _rowwise_call(
        _proj_a_kernel, xf, [g_mix, lw["w_a"], lw["qg"], lw["kg"], lw["seg"]],
        [(W_A, MXU_DTYPE), (W_IDX_Q, MXU_DTYPE), (2 * BLK, F32), (HEAD_DIM, MXU_DTYPE), (HEAD_DIM, MXU_DTYPE),
         (IDX_DIM, MXU_DTYPE)], "proj_a")
    qs, ks, vs, ksb, vsb = _rowwise_call(
        _proj_b_kernel, xf, [g_mix, lw["w_b"]],
        [(W_SB, MXU_DTYPE), (W_SB, F32), (W_SB, F32), (W_SB, MXU_DTYPE), (W_SB, MXU_DTYPE)], "proj_b")
    (gates,) = _rowwise_call(_proj_c_kernel, xf, [g_mix, lw["w_c"]], [(2 * d, F32)], "proj_c")

    r3 = lambda a: a.reshape(b, nq, a.shape[-1])
    qa, qi, kvk, ka, va, ki, qs, ks, vs, ksb, vsb = map(r3, (qa, qi, kvk, ka, va, ki, qs, ks, vs, ksb, vsb))

    if kv_prefix is None:
        ka_all, va_all, ki_all, ks_all, vs_all = ka, va, ki, ksb, vsb
    else:
        def join(prefix, new):
            cat = jnp.concatenate([prefix, new[:, :n_new]], axis=1)
            pad = (-cat.shape[1]) % BLK
            return jnp.pad(cat, ((0, 0), (0, pad), (0, 0)))
        ka_all, va_all, ki_all, ks_all, vs_all = (join(p, n_) for p, n_ in zip(kv_prefix, (ka, va, ki, ksb, vsb)))
    ya = _dsa_call(qa, qi, kvk, ka_all, ki_all, va_all, rbt, bidx,
                   qb0=qb0, n_valid=n_valid, chunk_off=chunk_off, topk=topk)
    ys = _sb_call(qs, ks_all, vs_all, qb0=qb0)
    x_mid = _merge_call(xf, ya.reshape(b * nq, W_A), ys.reshape(b * nq, W_SB), gates,
                        lw["w_pa"], lw["w_ps"], lw["w_o"]).reshape(b, nq, d)

    last = x_mid[:, n_new - (CONV_W - 1):n_new].reshape(b * (CONV_W - 1), d)
    (conv_rows,) = _rowwise_call(_proj_c_kernel, last, [lw["ln_ffn_g"], lw["w_up_a"]],
                                 [(lw["w_up_a"].shape[1], F32)], "conv_state")
    x_out = _ffn_call(x_mid, lw["ln_ffn_g"], conv_state, lw["w_up_a"], lw["w_up_u"], lw["conv_w"], lw["conv_b"],
                      lw["w_down"], _ffn_chunks(lw["w_up_a"].shape[1]))
    new_rows = dict(a_k=kvk[:, :n_new, 0:HEAD_DIM], a_v=kvk[:, :n_new, HEAD_DIM:2 * HEAD_DIM],
                    idx_k=kvk[:, :n_new, 2 * HEAD_DIM:2 * HEAD_DIM + IDX_DIM],
                    sb_k=ks[:, :n_new], sb_v=vs[:, :n_new],
                    conv=conv_rows.reshape(b, CONV_W - 1, -1))
    return x_out, new_rows


def kernel(x_prompt, x_sample, cache_a_k, cache_a_v, cache_idx_k, cache_sb_k, cache_sb_v, state_ffn_conv, meta_tokens, rel_bias, ln_mix_g, w_in, q_norm_g, k_norm_g, w_proj_a, w_proj_sb, w_out, ln_ffn_g, w_up, conv_w, conv_b, w_down):
    depth, d_model, _ = w_in.shape
    b_p, seq, _ = x_prompt.shape
    n_meta = meta_tokens.shape[0]
    b_s, n_s, _ = x_sample.shape
    past = cache_a_k.shape[2]
    d_ff = w_down.shape[1]
    n_p = n_meta + seq
    topk_p = min(TOPK_MAX, seq // 4)
    topk_s = min(TOPK_MAX, (past + n_s) // 4)
    assert n_meta <= CHUNK and CONV_W - 1 <= min(n_s, SUBLANES)

    sizes = (W_A, HEAD_DIM, HEAD_DIM, W_IDX_Q, IDX_DIM, N_IDX_HEADS, W_SB, W_SB, W_SB, d_model, d_model)
    offs = np.concatenate([[0], np.cumsum(sizes)])
    col = lambda k: w_in[:, :, offs[k]:offs[k + 1]]
    w_pad = jnp.zeros((depth, d_model, 2 * BLK - 3 * HEAD_DIM - N_IDX_HEADS), w_in.dtype)
    w_a = jnp.concatenate([col(0), col(3), col(1), col(2), col(4), col(5), w_pad], axis=-1).astype(MXU_DTYPE)
    w_b = w_in[:, :, offs[6]:offs[9]].astype(MXU_DTYPE)
    w_c = w_in[:, :, offs[9]:offs[11]].astype(MXU_DTYPE)
    seg_np = np.kron(np.eye(N_HEADS_A), np.full((HEAD_DIM, HEAD_DIM), 1.0 / HEAD_DIM))
    seg = jnp.asarray(seg_np, MXU_DTYPE)
    kg_pad = jnp.concatenate([k_norm_g, jnp.ones((depth, 2 * BLK - HEAD_DIM), k_norm_g.dtype)], axis=-1)
    rbt = rel_bias.T.astype(F32)
    bidx = jnp.asarray(_near_bucket_table())

    layers = []
    for l in range(depth):
        layers.append(dict(
            ln_mix_g=ln_mix_g[l][None], w_a=w_a[l], w_b=w_b[l], w_c=w_c[l],
            qg=jnp.tile(q_norm_g[l], N_HEADS_A)[None], kg=kg_pad[l][None], seg=seg,
            w_pa=w_proj_a[l].astype(MXU_DTYPE), w_ps=w_proj_sb[l].astype(MXU_DTYPE), w_o=w_out[l].astype(MXU_DTYPE),
            ln_ffn_g=ln_ffn_g[l][None], w_up_a=w_up[l][:, :d_ff].astype(MXU_DTYPE),
            w_up_u=w_up[l][:, d_ff:].astype(MXU_DTYPE), conv_w=conv_w[l], conv_b=conv_b[l][None],
            w_down=w_down[l].astype(MXU_DTYPE)))

    np_pad = -(-n_p // BLK) * BLK
    meta = jnp.broadcast_to(meta_tokens.astype(x_prompt.dtype)[None], (b_p, n_meta, d_model))
    xp = jnp.concatenate([meta, x_prompt, jnp.zeros((b_p, np_pad - n_p, d_model), x_prompt.dtype)], axis=1)
    zero_state = jnp.zeros((b_p, SUBLANES, d_ff), F32)

    ns_pad = -(-n_s // BLK) * BLK
    assert past % BLK == 0 and ns_pad == BLK
    xs = jnp.pad(x_sample, ((0, 0), (0, ns_pad - n_s), (0, 0)))

    outs_p, outs_s = [], []
    for l in range(depth):
        lw = layers[l]
        xp, rows_p = _layer(xp, None, zero_state, lw, n_new=n_p, qb0=0, n_valid=n_p, chunk_off=n_meta,
                            topk=topk_p, rbt=rbt, bidx=bidx)
        outs_p.append(rows_p)

        prefix = (cache_a_k[l].reshape(b_s, past, HEAD_DIM).astype(MXU_DTYPE),
                  cache_a_v[l].reshape(b_s, past, HEAD_DIM).astype(MXU_DTYPE),
                  cache_idx_k[l].astype(MXU_DTYPE),
                  cache_sb_k[l].reshape(b_s, past, W_SB).astype(MXU_DTYPE),
                  cache_sb_v[l].reshape(b_s, past, W_SB).astype(MXU_DTYPE))
        st = jnp.pad(state_ffn_conv[l].astype(F32), ((0, 0), (SUBLANES - (CONV_W - 1), 0), (0, 0)))
        xs, rows_s = _layer(xs, prefix, st, lw, n_new=n_s, qb0=past // BLK, n_valid=past + n_s, chunk_off=0,
                            topk=topk_s, rbt=rbt, bidx=bidx)
        outs_s.append(rows_s)

    def stack(outs, name, shape_tail):
        a = jnp.stack([o[name] for o in outs])
        return a.reshape(a.shape[:3] + shape_tail)

    def group(outs):
        return (stack(outs, "a_k", (1, HEAD_DIM)), stack(outs, "a_v", (1, HEAD_DIM)), stack(outs, "idx_k", (IDX_DIM,)),
                stack(outs, "sb_k", (N_HEADS_SB, SB_HEAD_DIM)), stack(outs, "sb_v", (N_HEADS_SB, SB_HEAD_DIM)),
                stack(outs, "conv", (d_ff,)))

    y_prompt = xp[:, n_meta:n_p]
    y_sample = xs[:, :n_s]
    return (y_prompt, y_sample) + group(outs_p) + group(outs_s)
```

```python
import functools
import math

import numpy as np
import jax
import jax.numpy as jnp
from jax import lax
from jax.experimental import pallas as pl
from jax.experimental.pallas import tpu as pltpu

CHUNK = 64
HEAD_DIM = 64
N_HEADS_A = 8
N_IDX_HEADS = 4
IDX_DIM = 64
TOPK_MAX = 256
N_HEADS_SB = 4
SB_HEAD_DIM = 128
N_BUCKETS = 32
MAX_DISTANCE = 128
CONV_W = 3
EPS = 1e-6

W_A = N_HEADS_A * HEAD_DIM
W_IDX_Q = N_IDX_HEADS * IDX_DIM
W_SB = N_HEADS_SB * SB_HEAD_DIM

BLK = 128
DSA_UNROLL = 4
SCORE_UNROLL = 2 * DSA_UNROLL
SB_Q_BLOCKS = 2
DENOM_ROWS = 16
LOG2E = math.log2(math.e)
SUBLANES = 8
ROW_TILE_MAX = 640
VMEM_LIMIT = 56 * 1024 * 1024
MXU_DTYPE = jnp.bfloat16
INT_MIN = -2 ** 31
SB_UNDERFLOW = -104.0

F32 = jnp.float32


def _cparams(sem):
    return pltpu.CompilerParams(dimension_semantics=sem, vmem_limit_bytes=VMEM_LIMIT)


def _row_tile(n_rows, cap=ROW_TILE_MAX):
    if n_rows <= BLK:
        return n_rows
    best = BLK
    t = BLK
    while t <= min(cap, n_rows):
        if n_rows % t == 0:
            best = t
        t += BLK
    return best


def _rms(x, g):
    return x * lax.rsqrt(jnp.mean(x * x, axis=-1, keepdims=True) + EPS) * g


def _dot(a, b):
    return jnp.dot(a, b, preferred_element_type=F32)


def _split_dot(a, b):
    hi = a.astype(MXU_DTYPE)
    lo = (a - hi.astype(F32)).astype(MXU_DTYPE)
    return _dot(hi, b) + _dot(lo, b)


def _proj_a_kernel(x_ref, g_ref, w_ref, qg_ref, kg_ref, seg_ref,
                   qa_ref, qi_ref, kvk_ref, ka_ref, va_ref, ki_ref):
    xn = _rms(x_ref[...], g_ref[...])
    y = _dot(xn.astype(MXU_DTYPE), w_ref[...])
    q = y[:, :W_A]
    ms = _split_dot(q * q, seg_ref[...])
    qa_ref[...] = (q * lax.rsqrt(ms + EPS) * qg_ref[...]).astype(qa_ref.dtype)
    qi_ref[...] = y[:, W_A:W_A + W_IDX_Q].astype(qi_ref.dtype)
    kvk = y[:, W_A + W_IDX_Q:]
    lane = lax.broadcasted_iota(jnp.int32, kvk.shape, 1)
    is_k = lane < HEAD_DIM
    msk = jnp.sum(jnp.where(is_k, kvk * kvk, 0.0), axis=-1, keepdims=True) * (1.0 / HEAD_DIM)
    kvk = jnp.where(is_k, kvk * lax.rsqrt(msk + EPS) * kg_ref[...], kvk)
    kvk_ref[...] = kvk
    ka_ref[...] = kvk[:, 0:HEAD_DIM].astype(ka_ref.dtype)
    va_ref[...] = kvk[:, HEAD_DIM:2 * HEAD_DIM].astype(va_ref.dtype)
    ki_ref[...] = kvk[:, 2 * HEAD_DIM:2 * HEAD_DIM + IDX_DIM].astype(ki_ref.dtype)


def _proj_b_kernel(x_ref, g_ref, w_ref, qs_ref, ks_ref, vs_ref, ksb_ref, vsb_ref):
    xn = _rms(x_ref[...], g_ref[...])
    y = _dot(xn.astype(MXU_DTYPE), w_ref[...])
    qs_ref[...] = y[:, :W_SB].astype(qs_ref.dtype)
    k = y[:, W_SB:2 * W_SB]
    v = y[:, 2 * W_SB:]
    ks_ref[...] = k
    vs_ref[...] = v
    ksb_ref[...] = k.astype(ksb_ref.dtype)
    vsb_ref[...] = v.astype(vsb_ref.dtype)


def _proj_c_kernel(x_ref, g_ref, w_ref, o_ref):
    xn = _rms(x_ref[...], g_ref[...])
    o_ref[...] = _dot(xn.astype(MXU_DTYPE), w_ref[...])


def _rowwise_call(kernel, x, consts, out_cols_dtypes, name):
    m, d = x.shape
    tm = _row_tile(m)
    in_specs = [pl.BlockSpec((tm, d), lambda i: (i, 0))]
    for c in consts:
        in_specs.append(pl.BlockSpec(c.shape, lambda i, nd=c.ndim: (0,) * nd))
    out_shape = [jax.ShapeDtypeStruct((m, n), dt) for n, dt in out_cols_dtypes]
    out_specs = [pl.BlockSpec((tm, n), lambda i: (i, 0)) for n, _ in out_cols_dtypes]
    return pl.pallas_call(
        kernel, grid=(m // tm,), in_specs=in_specs, out_specs=out_specs, out_shape=out_shape,
        compiler_params=_cparams(("arbitrary",)), name=name)(x, *consts)


def _t5_bucket_np(rel):
    half = N_BUCKETS // 2
    max_exact = half // 2
    n = np.abs(rel)
    large = np.full(n.shape, max_exact, dtype=np.int64)
    steps = half - max_exact
    for t in range(1, steps + 1):
        lhs = n.astype(object) ** steps
        rhs = (max_exact ** steps) * ((MAX_DISTANCE // max_exact) ** t)
        large = large + (np.array(lhs >= rhs, dtype=bool)).astype(np.int64)
    large = np.minimum(large, half - 1)
    return np.where(rel > 0, half, 0) + np.where(n < max_exact, n, large)


def _near_bucket_table():
    m = np.arange(2 * BLK)
    rows = [_t5_bucket_np(d * BLK + BLK - m) for d in (-1, 0, 1)]
    return np.stack(rows).astype(np.int32)


def _pair_transpose(x, n_pairs):
    cols = []
    for p in range(n_pairs):
        t = x[:, p * BLK:(p + 1) * BLK].T
        cols.append(t[:HEAD_DIM])
        cols.append(t[HEAD_DIM:])
    return jnp.concatenate(cols, axis=1)


def _dsa_kernel(qa_ref, qi_ref, kvk_ref, ka_ref, ki_ref, vat_ref, rbt_ref, bidx_ref, o_ref,
                key_scr, bias_scr, m_scr, acc_scr, tie_scr, lg0_scr, lg1_scr, cm0_scr, cm1_scr,
                *, qb0, nsb_total, n_valid, chunk_off, topk):
    i = pl.program_id(1) + qb0
    nsb = jnp.minimum((i + 2 + DSA_UNROLL - 1) // DSA_UNROLL, nsb_total)
    nh = N_HEADS_A
    sbk = DSA_UNROLL * BLK

    qat = (_pair_transpose(qa_ref[...].astype(F32), nh // 2) * (HEAD_DIM ** -0.5 * LOG2E)).astype(MXU_DTYPE)
    qit = _pair_transpose(qi_ref[...].astype(F32), N_IDX_HEADS // 2).astype(MXU_DTYPE)
    kw_t = kvk_ref[...][:, BLK:2 * BLK].T
    w_scale = (IDX_DIM ** -0.5) * (N_IDX_HEADS ** -0.5)
    w_rows = [kw_t[IDX_DIM + h:IDX_DIM + h + 1, :] * w_scale for h in range(N_IDX_HEADS)]

    qpos = i * BLK + lax.broadcasted_iota(jnp.int32, (1, BLK), 1)
    kend = (((qpos + (CHUNK - chunk_off)) >> 6) << 6) + chunk_off
    kend = jnp.minimum(kend, n_valid)
    krow = lax.broadcasted_iota(jnp.int32, (sbk, BLK), 0)

    @pl.when(jnp.logical_and(pl.program_id(0) == 0, pl.program_id(1) == 0))
    def _():
        rbt = rbt_ref[...]
        far_bucket = N_BUCKETS // 2 - 1
        cfar = rbt[:, far_bucket:far_bucket + 1]
        bias_scr[0] = jnp.zeros((BLK, nh * BLK), F32)
        for d in range(3):
            idx = bidx_ref[d:d + 1, :]
            tab = jnp.zeros((nh, 2 * BLK), F32)
            for b in range(N_BUCKETS):
                tab = jnp.where(idx == b, rbt[:, b:b + 1], tab)
            tab = (tab - cfar) * LOG2E
            for h in range(nh):
                trow = jnp.broadcast_to(tab[h:h + 1, :], (BLK, 2 * BLK))
                bias_scr[d + 1, :, h * BLK:(h + 1) * BLK] = pltpu.roll(trow, 0, 1, stride=1, stride_axis=0)[:, BLK:]

    def to_key(v):
        bits = lax.bitcast_convert_type(v, jnp.int32)
        key = bits ^ ((bits >> 31) & 0x7FFFFFFF)
        return jnp.where(v == 0.0, 0, key)

    def from_key(key):
        return lax.bitcast_convert_type(key ^ ((key >> 31) & 0x7FFFFFFF), F32)

    sck = SCORE_UNROLL * BLK
    srow = lax.broadcasted_iota(jnp.int32, (sck, BLK), 0)

    def score_block(s_, carry):
        smin, smax = carry
        off = pl.multiple_of(s_ * sck, sck)
        s = jnp.maximum(_dot(ki_ref[pl.ds(off, sck), :], qit), 0.0)
        sc = s[:, 0:BLK] * w_rows[0]
        for h in range(1, N_IDX_HEADS):
            sc = sc + s[:, h * BLK:(h + 1) * BLK] * w_rows[h]
        adm = (srow + off) < kend
        smin = jnp.minimum(smin, jnp.min(jnp.where(adm, sc, jnp.inf), axis=0, keepdims=True))
        smax = jnp.maximum(smax, jnp.max(jnp.where(adm, sc, -jnp.inf), axis=0, keepdims=True))
        key_scr[pl.ds(off, sck), :] = jnp.where(adm, to_key(sc), INT_MIN)
        return smin, smax

    n_score = (nsb * sbk + sck - 1) // sck
    smin, smax = lax.fori_loop(0, n_score, score_block,
                               (jnp.full((1, BLK), jnp.inf, F32), jnp.full((1, BLK), -jnp.inf, F32)))

    n_acc = 8 * SUBLANES

    def count(thrs):
        def body(s_, accs):
            off = pl.multiple_of(s_ * sbk, sbk)
            blk = key_scr[pl.ds(off, sbk), :]
            return tuple(acc + jnp.sum(jnp.where(blk >= t, 1.0, 0.0).reshape(sbk // n_acc, n_acc, BLK), axis=0)
                         for acc, t in zip(accs, thrs))
        accs = lax.fori_loop(0, nsb, body, tuple(jnp.zeros((n_acc, BLK), F32) for _ in thrs))
        return [jnp.sum(acc, axis=0, keepdims=True) for acc in accs]

    kf = float(topk)
    n_adm = kend.astype(F32)
    few = n_adm < kf

    def narrow(state, cand, cnt, live):
        lo, c_lo, hi, c_hi = state
        up = cnt >= kf
        take_lo = live & up & (cand > lo)
        take_hi = live & jnp.logical_not(up) & (cand < hi)
        return (jnp.where(take_lo, cand, lo), jnp.where(take_lo, cnt, c_lo),
                jnp.where(take_hi, cand, hi), jnp.where(take_hi, cnt, c_hi))

    def is_open(state):
        lo, c_lo, hi, c_hi = state
        return jnp.logical_not(few) & (c_lo != kf) & (hi - 1 > lo) & (c_lo - c_hi > 2.0)

    state = (to_key(smin), n_adm, to_key(smax) + 1, jnp.zeros((1, BLK), F32))
    zero = jnp.zeros((1, BLK), jnp.int32)
    c_nonneg, c_pos = count([zero, zero + 1])
    all_lanes = jnp.logical_not(few)
    state = narrow(state, zero, c_nonneg, all_lanes)
    state = narrow(state, zero + 1, c_pos, all_lanes)

    group = 2
    max_passes = 4 * 33

    def n_open_of(state):
        return jnp.sum(jnp.where(is_open(state), 1.0, 0.0))

    def search_cond(c):
        t, n_open = c[0], c[1]
        return jnp.logical_and(t < max_passes, n_open > 0.0)

    def search_steps(c):
        t, _, state = c[0], c[1], c[2:]
        for u in range(group):
            lo, _, hi, _ = state
            mid_v = to_key(0.5 * from_key(lo) + 0.5 * from_key(hi - 1))
            mid_k = (lo >> 1) + (hi >> 1) + (lo & hi & 1)
            cand = jnp.where((t + u) % 4 == 3, mid_k, mid_v)
            cand = jnp.minimum(jnp.maximum(cand, lo + 1), hi - 1)
            (cnt,) = count([cand])
            state = narrow(state, cand, cnt, is_open(state))
        return (t + group, n_open_of(state)) + tuple(state)

    res = lax.while_loop(search_cond, search_steps, (jnp.int32(0), n_open_of(state)) + tuple(state))
    lo, c_lo, hi, c_hi = res[2:]
    exact_k = c_lo == kf
    exact_t = jnp.logical_not(exact_k) & jnp.logical_not(hi - 1 > lo)
    pair = jnp.logical_not(few | exact_k | exact_t)

    def largest_below(bound):
        def body(s_, acc):
            off = pl.multiple_of(s_ * sbk, sbk)
            blk = key_scr[pl.ds(off, sbk), :]
            return jnp.maximum(acc, jnp.max(jnp.where(blk < bound, blk, INT_MIN).reshape(sbk // n_acc, n_acc, BLK),
                                            axis=0))
        acc = lax.fori_loop(0, nsb, body, jnp.full((n_acc, BLK), INT_MIN, jnp.int32))
        rows = n_acc
        while rows > SUBLANES:
            rows //= 2
            acc = jnp.maximum(acc[:rows], acc[rows:2 * rows])
        for shift in (4, 2, 1):
            acc = jnp.maximum(acc, pltpu.roll(acc, shift, 0))
        return acc[0:1]

    kth = lax.cond(jnp.sum(jnp.where(pair, 1.0, 0.0)) > 0.0, lambda: largest_below(hi), lambda: lo)
    thr = jnp.where(few, INT_MIN, jnp.where(exact_k, lo - 1, jnp.where(exact_t, lo, kth)))
    n_tie_keep = jnp.where(few | exact_k, 0.0, kf - c_hi)

    m_scr[...] = jnp.full(m_scr.shape, -jnp.inf, F32)
    acc_scr[...] = jnp.zeros(acc_scr.shape, F32)
    tie_scr[...] = jnp.zeros(tie_scr.shape, F32)
    lrow = lax.broadcasted_iota(jnp.int32, (BLK, BLK), 0)
    lcol = lax.broadcasted_iota(jnp.int32, (BLK, BLK), 1)
    lstrict = jnp.where(lcol < lrow, 1.0, 0.0).astype(MXU_DTYPE)

    def stage_a(s_next, lg_ref, cm_ref, with_bias):
        s_ = jnp.minimum(s_next, nsb - 1)
        off = pl.multiple_of(s_ * sbk, sbk)
        masked = jnp.where(s_next < nsb, 0.0, -jnp.inf)
        tie_run = tie_scr[0:1, :]
        selb = []
        for u in range(DSA_UNROLL):
            key = key_scr[pl.ds(off + u * BLK, BLK), :]
            eq = key == thr
            eqf = jnp.where(eq, 1.0, 0.0)
            ties_before = _dot(lstrict, eqf.astype(MXU_DTYPE)) + tie_run
            sel = (key > thr) | (eq & (ties_before < n_tie_keep))
            tie_run = tie_run + jnp.sum(eqf, axis=0, keepdims=True)
            selb.append(jnp.where(sel, masked, -jnp.inf))
        tie_scr[0:1, :] = tie_run
        kab = ka_ref[pl.ds(off, sbk), :]
        tile_idx = [jnp.clip(s_ * DSA_UNROLL + u - i, -2, 1) + 2 for u in range(DSA_UNROLL)]
        for hp in range(nh // 2):
            lg = _dot(kab, qat[:, hp * 2 * BLK:(hp + 1) * 2 * BLK])
            for hh in range(2):
                h = 2 * hp + hh
                cols = slice(h * BLK, (h + 1) * BLK)
                col_max = None
                for u in range(DSA_UNROLL):
                    rows = slice(u * BLK, (u + 1) * BLK)
                    piece = lg[rows, hh * BLK:(hh + 1) * BLK] + selb[u]
                    if with_bias:
                        piece = piece + bias_scr[tile_idx[u], :, cols]
                    lg_ref[rows, cols] = piece
                    part = jnp.max(piece.reshape(BLK // SUBLANES, SUBLANES, BLK), axis=0)
                    col_max = part if col_max is None else jnp.maximum(col_max, part)
                cm_ref[:, cols] = col_max

    def stage_b(s_, lg_ref, cm_ref):
        m_old = m_scr[0:1, :]
        m_new = jnp.maximum(m_old, jnp.max(cm_ref[...], axis=0, keepdims=True))
        m_safe = jnp.where(m_new == -jnp.inf, 0.0, m_new)
        alpha = jnp.exp2(m_old - m_safe)
        p = jnp.exp2(lg_ref[...] - m_safe)
        vat = vat_ref[jnp.minimum(s_, nsb_total - 1)]
        acc_scr[...] = acc_scr[...] * alpha + _dot(vat, p.astype(MXU_DTYPE))
        m_scr[0:1, :] = m_new

    def pair_body(pi, with_bias):
        s_ = 2 * pi
        stage_a(s_ + 1, lg1_scr, cm1_scr, with_bias)
        stage_b(s_, lg0_scr, cm0_scr)
        stage_a(s_ + 2, lg0_scr, cm0_scr, with_bias)
        stage_b(s_ + 1, lg1_scr, cm1_scr)

    def far_pair(pi, carry):
        pair_body(pi, False)
        return carry

    def near_pair(pi, carry):
        pair_body(pi, True)
        return carry

    n_far_sb = jnp.maximum(i - 1, 0) // DSA_UNROLL
    n_far_pairs = jnp.maximum(n_far_sb - 1, 0) // 2
    stage_a(0, lg0_scr, cm0_scr, True)
    lax.fori_loop(0, n_far_pairs, far_pair, 0)
    lax.fori_loop(n_far_pairs, nsb // 2, near_pair, 0)

    @pl.when(nsb % 2 == 1)
    def _():
        stage_b(nsb - 1, lg0_scr, cm0_scr)

    out_t = acc_scr[0:HEAD_DIM, :] / acc_scr[HEAD_DIM:HEAD_DIM + 1, :]
    for p in range(nh // 2):
        pair = jnp.concatenate([out_t[:, (2 * p) * BLK:(2 * p + 1) * BLK],
                                out_t[:, (2 * p + 1) * BLK:(2 * p + 2) * BLK]], axis=0)
        o_ref[:, p * BLK:(p + 1) * BLK] = pair.T.astype(o_ref.dtype)


def _dsa_call(qa, qi, kvk, ka, ki, va, rbt, bidx, *, qb0, n_valid, chunk_off, topk):
    b, nq, _ = qa.shape
    sbk = DSA_UNROLL * BLK
    pad = (-ka.shape[1]) % (SCORE_UNROLL * BLK)
    ka, ki, va = (jnp.pad(a, ((0, 0), (0, pad), (0, 0))) for a in (ka, ki, va))
    nk = ka.shape[1]
    nsb_total = nk // sbk
    vat = va.reshape(b, nsb_total, sbk, HEAD_DIM).transpose(0, 1, 3, 2)
    vat = jnp.concatenate([vat, jnp.ones((b, nsb_total, DENOM_ROWS, sbk), vat.dtype)], axis=2)
    nh = N_HEADS_A
    kernel = functools.partial(_dsa_kernel, qb0=qb0, nsb_total=nsb_total, n_valid=n_valid,
                               chunk_off=chunk_off, topk=topk)
    qspec = lambda w: pl.BlockSpec((None, BLK, w), lambda bi, qi_: (bi, qi_, 0))
    full = lambda a: pl.BlockSpec((None,) + a.shape[1:], lambda bi, qi_, nd=a.ndim: (bi,) + (0,) * (nd - 1))
    const = lambda a: pl.BlockSpec(a.shape, lambda bi, qi_, nd=a.ndim: (0,) * nd)
    return pl.pallas_call(
        kernel, grid=(b, nq // BLK),
        in_specs=[qspec(W_A), qspec(W_IDX_Q), qspec(2 * BLK), full(ka), full(ki), full(vat), const(rbt), const(bidx)],
        out_specs=qspec(W_A),
        out_shape=jax.ShapeDtypeStruct((b, nq, W_A), MXU_DTYPE),
        scratch_shapes=[
            pltpu.VMEM((nk, BLK), jnp.int32),
            pltpu.VMEM((4, BLK, nh * BLK), F32),
            pltpu.VMEM((SUBLANES, nh * BLK), F32),
            pltpu.VMEM((HEAD_DIM + DENOM_ROWS, nh * BLK), F32),
            pltpu.VMEM((SUBLANES, BLK), F32),
            pltpu.VMEM((sbk, nh * BLK), F32),
            pltpu.VMEM((sbk, nh * BLK), F32),
            pltpu.VMEM((SUBLANES, nh * BLK), F32),
            pltpu.VMEM((SUBLANES, nh * BLK), F32),
        ],
        compiler_params=_cparams(("arbitrary", "arbitrary")), name="dsa")(qa, qi, kvk, ka, ki, vat, rbt, bidx)


def _sb_kernel(q_ref, k_ref, v_ref, o_ref, *, qb0, nkb_total):
    bq = q_ref.shape[0]
    nbq = bq // BLK
    i0 = qb0 + pl.program_id(1) * nbq
    j_top = jnp.minimum(i0 + nbq - 1, nkb_total - 1)
    qpos = i0 * BLK + lax.broadcasted_iota(jnp.int32, (bq, BLK), 0)
    kcol = lax.broadcasted_iota(jnp.int32, (bq, BLK), 1)
    trow = lax.broadcasted_iota(jnp.int32, (BLK, BLK), 0)
    tcol = lax.broadcasted_iota(jnp.int32, (BLK, BLK), 1)
    tri = jnp.where(trow > tcol, 1.0, 0.0).astype(MXU_DTYPE)
    scale = SB_HEAD_DIM ** -0.5
    hd = SB_HEAD_DIM
    qs = [q_ref[:, h * hd:(h + 1) * hd] for h in range(N_HEADS_SB)]

    def block(j, later_blocks, acc, masked):
        off = pl.multiple_of(j * BLK, BLK)
        if masked:
            before = (kcol + off) < qpos
        new_lb, new_acc = [], []
        for h in range(N_HEADS_SB):
            kb = k_ref[pl.ds(off, BLK), h * hd:(h + 1) * hd]
            vb = v_ref[pl.ds(off, BLK), h * hd:(h + 1) * hd]
            z = lax.dot_general(qs[h], kb, (((1,), (1,)), ((), ())), preferred_element_type=F32) * scale
            log_keep = -(jnp.maximum(z, 0.0) + jnp.log1p(jnp.exp(-jnp.abs(z))))
            if masked:
                log_keep = jnp.where(before, log_keep, 0.0)
            later = _split_dot(log_keep, tri) + later_blocks[h]
            a = jnp.exp(log_keep + z + later)
            if masked:
                a = jnp.where(before, a, 0.0)
            new_acc.append(acc[h] + _dot(a.astype(MXU_DTYPE), vb))
            new_lb.append(later_blocks[h] + jnp.sum(log_keep, axis=1, keepdims=True))
        return tuple(new_lb), tuple(new_acc)

    def worst(lb):
        return jnp.max(functools.reduce(jnp.maximum, lb))

    later_blocks = tuple(jnp.zeros((bq, 1), F32) for _ in range(N_HEADS_SB))
    acc = tuple(jnp.zeros((bq, hd), F32) for _ in range(N_HEADS_SB))
    for t in range(nbq):
        later_blocks, acc = block(jnp.maximum(j_top - t, 0), later_blocks, acc, True)

    def cond(c):
        j, w, _, _ = c
        return jnp.logical_and(j >= 0, w > SB_UNDERFLOW)

    def body(c):
        j, _, lb, ac = c
        lb, ac = block(j, lb, ac, False)
        return j - 1, worst(lb), lb, ac

    _, _, _, acc = lax.while_loop(cond, body, (j_top - nbq, worst(later_blocks), later_blocks, acc))
    for h in range(N_HEADS_SB):
        o_ref[:, h * hd:(h + 1) * hd] = acc[h].astype(o_ref.dtype)


def _sb_call(q, k, v, *, qb0):
    b, nq, _ = q.shape
    nk = k.shape[1]
    nbq = SB_Q_BLOCKS if nq > BLK else 1
    bq = nbq * BLK
    pad = (-nq) % bq
    if pad:
        q = jnp.pad(q, ((0, 0), (0, pad), (0, 0)))
    kernel = functools.partial(_sb_kernel, qb0=qb0, nkb_total=nk // BLK)
    kv_spec = pl.BlockSpec((None, nk, W_SB), lambda bi, qi_: (bi, 0, 0))
    q_spec = pl.BlockSpec((None, bq, W_SB), lambda bi, qi_: (bi, qi_, 0))
    out = pl.pallas_call(
        kernel, grid=(b, (nq + pad) // bq),
        in_specs=[q_spec, kv_spec, kv_spec], out_specs=q_spec,
        out_shape=jax.ShapeDtypeStruct((b, nq + pad, W_SB), MXU_DTYPE),
        compiler_params=_cparams(("arbitrary", "arbitrary")), name="sb")(q, k, v)
    return out[:, :nq] if pad else out


def _merge_kernel(x_ref, ya_ref, ys_ref, g_ref, wpa_ref, wps_ref, wo_ref, o_ref):
    d = x_ref.shape[-1]
    g = g_ref[...]
    m = (jax.nn.sigmoid(g[:, :d]) * _dot(ya_ref[...], wpa_ref[...])
         + jax.nn.sigmoid(g[:, d:]) * _dot(ys_ref[...], wps_ref[...]))
    o_ref[...] = x_ref[...] + _dot(m.astype(MXU_DTYPE), wo_ref[...])


def _merge_call(x, ya, ys, g, wpa, wps, wo):
    m, d = x.shape
    tm = _row_tile(m)
    row = lambda w: pl.BlockSpec((tm, w), lambda i: (i, 0))
    const = lambda a: pl.BlockSpec(a.shape, lambda i, nd=a.ndim: (0,) * nd)
    return pl.pallas_call(
        _merge_kernel, grid=(m // tm,),
        in_specs=[row(d), row(ya.shape[1]), row(ys.shape[1]), row(g.shape[1]), const(wpa), const(wps), const(wo)],
        out_specs=row(d), out_shape=jax.ShapeDtypeStruct((m, d), F32),
        compiler_params=_cparams(("arbitrary",)), name="merge")(x, ya, ys, g, wpa, wps, wo)


def _ffn_kernel(x_ref, g_ref, st_ref, wa_ref, wu_ref, cw_ref, cb_ref, wd_ref, o_ref,
                h_scr, carry_scr, act_scr):
    t = pl.program_id(1)
    c = pl.program_id(2)
    tm = x_ref.shape[0]

    @pl.when(c == 0)
    def _():
        x = x_ref[...]
        h_scr[...] = _rms(x, g_ref[...]).astype(h_scr.dtype)
        o_ref[...] = x

    @pl.when(t == 0)
    def _():
        carry_scr[c] = st_ref[...]

    h = h_scr[...]
    a = _dot(h, wa_ref[...])
    u = _dot(h, wu_ref[...])
    w0, w1, w2 = cw_ref[0:1, :], cw_ref[1:2, :], cw_ref[2:3, :]
    bias = cb_ref[...]

    def gate(a2, a1, a0, uu):
        cv = bias + (w0 * a2 + w1 * a1 + w2 * a0)
        return cv * jax.nn.sigmoid(cv) * uu

    act_scr[...] = gate(pltpu.roll(a, 2, 0), pltpu.roll(a, 1, 0), a, u)
    prev = carry_scr[c]
    p2 = prev[SUBLANES - 2:SUBLANES - 1, :]
    p1 = prev[SUBLANES - 1:SUBLANES, :]
    top = a[0:SUBLANES]
    rid = lax.broadcasted_iota(jnp.int32, top.shape, 0)
    a1 = jnp.where(rid == 0, p1, pltpu.roll(top, 1, 0))
    a2 = jnp.where(rid == 0, p2, jnp.where(rid == 1, p1, pltpu.roll(top, 2, 0)))
    act_scr[0:SUBLANES, :] = gate(a2, a1, top, u[0:SUBLANES])
    carry_scr[c] = a[tm - SUBLANES:tm]
    o_ref[...] += _dot(act_scr[...].astype(MXU_DTYPE), wd_ref[...])


def _ffn_call(x, g, state, wa, wu, cw, cb, wd, n_chunks):
    b, n, d = x.shape
    dff = wa.shape[1]
    cwid = dff // n_chunks
    tm = _row_tile(n)
    return pl.pallas_call(
        _ffn_kernel, grid=(b, n // tm, n_chunks),
        in_specs=[
            pl.BlockSpec((None, tm, d), lambda bi, t, c: (bi, t, 0)),
            pl.BlockSpec((1, d), lambda bi, t, c: (0, 0)),
            pl.BlockSpec((None, SUBLANES, cwid), lambda bi, t, c: (bi, 0, c)),
            pl.BlockSpec((d, cwid), lambda bi, t, c: (0, c)),
            pl.BlockSpec((d, cwid), lambda bi, t, c: (0, c)),
            pl.BlockSpec((CONV_W, cwid), lambda bi, t, c: (0, c)),
            pl.BlockSpec((1, cwid), lambda bi, t, c: (0, c)),
            pl.BlockSpec((cwid, d), lambda bi, t, c: (c, 0)),
        ],
        out_specs=pl.BlockSpec((None, tm, d), lambda bi, t, c: (bi, t, 0)),
        out_shape=jax.ShapeDtypeStruct((b, n, d), F32),
        scratch_shapes=[
            pltpu.VMEM((tm, d), MXU_DTYPE),
            pltpu.VMEM((n_chunks, SUBLANES, cwid), F32),
            pltpu.VMEM((tm, cwid), F32),
        ],
        compiler_params=_cparams(("arbitrary", "arbitrary", "arbitrary")), name="ffn")(
            x, g, state, wa, wu, cw, cb, wd)


def _ffn_chunks(dff):
    return 2 if dff % (2 * BLK) == 0 else 1


def _layer(x, kv_prefix, conv_state, lw, *, n_new, qb0, n_valid, chunk_off, topk, rbt, bidx):
    b, nq, d = x.shape
    xf = x.reshape(b * nq, d)
    g_mix = lw["ln_mix_g"]
    qa, qi, kvk, ka, va, ki = _rowwise_call(
        _proj_a_kernel, xf, [g_mix, lw["w_a"], lw["qg"], lw["kg"], lw["seg"]],
        [(W_A, MXU_DTYPE), (W_IDX_Q, MXU_DTYPE), (2 * BLK, F32), (HEAD_DIM, MXU_DTYPE), (HEAD_DIM, MXU_DTYPE),
         (IDX_DIM, MXU_DTYPE)], "proj_a")
    qs, ks, vs, ksb, vsb = _rowwise_call(
        _proj_b_kernel, xf, [g_mix, lw["w_b"]],
        [(W_SB, MXU_DTYPE), (W_SB, F32), (W_SB, F32), (W_SB, MXU_DTYPE), (W_SB, MXU_DTYPE)], "proj_b")
    (gates,) = _rowwise_call(_proj_c_kernel, xf, [g_mix, lw["w_c"]], [(2 * d, F32)], "proj_c")

    r3 = lambda a: a.reshape(b, nq, a.shape[-1])
    qa, qi, kvk, ka, va, ki, qs, ks, vs, ksb, vsb = map(r3, (qa, qi, kvk, ka, va, ki, qs, ks, vs, ksb, vsb))

    if kv_prefix is None:
        ka_all, va_all, ki_all, ks_all, vs_all = ka, va, ki, ksb, vsb
    else:
        def join(prefix, new):
            cat = jnp.concatenate([prefix, new[:, :n_new]], axis=1)
            pad = (-cat.shape[1]) % BLK
            return jnp.pad(cat, ((0, 0), (0, pad), (0, 0)))
        ka_all, va_all, ki_all, ks_all, vs_all = (join(p, n_) for p, n_ in zip(kv_prefix, (ka, va, ki, ksb, vsb)))
    ya = _dsa_call(qa, qi, kvk, ka_all, ki_all, va_all, rbt, bidx,
                   qb0=qb0, n_valid=n_valid, chunk_off=chunk_off, topk=topk)
    ys = _sb_call(qs, ks_all, vs_all, qb0=qb0)
    x_mid = _merge_call(xf, ya.reshape(b * nq, W_A), ys.reshape(b * nq, W_SB), gates,
                        lw["w_pa"], lw["w_ps"], lw["w_o"]).reshape(b, nq, d)

    last = x_mid[:, n_new - (CONV_W - 1):n_new].reshape(b * (CONV_W - 1), d)
    (conv_rows,) = _rowwise_call(_proj_c_kernel, last, [lw["ln_ffn_g"], lw["w_up_a"]],
                                 [(lw["w_up_a"].shape[1], F32)], "conv_state")
    x_out = _ffn_call(x_mid, lw["ln_ffn_g"], conv_state, lw["w_up_a"], lw["w_up_u"], lw["conv_w"], lw["conv_b"],
                      lw["w_down"], _ffn_chunks(lw["w_up_a"].shape[1]))
    new_rows = dict(kvk=kvk, sb_k=ks, sb_v=vs, conv=conv_rows.reshape(b, CONV_W - 1, -1))
    return x_out, new_rows


def kernel(x_prompt, x_sample, cache_a_k, cache_a_v, cache_idx_k, cache_sb_k, cache_sb_v, state_ffn_conv, meta_tokens, rel_bias, ln_mix_g, w_in, q_norm_g, k_norm_g, w_proj_a, w_proj_sb, w_out, ln_ffn_g, w_up, conv_w, conv_b, w_down):
    depth, d_model, _ = w_in.shape
    b_p, seq, _ = x_prompt.shape
    n_meta = meta_tokens.shape[0]
    b_s, n_s, _ = x_sample.shape
    past = cache_a_k.shape[2]
    d_ff = w_down.shape[1]
    n_p = n_meta + seq
    topk_p = min(TOPK_MAX, seq // 4)
    topk_s = min(TOPK_MAX, (past + n_s) // 4)
    assert n_meta <= CHUNK and CONV_W - 1 <= min(n_s, SUBLANES)

    sizes = (W_A, HEAD_DIM, HEAD_DIM, W_IDX_Q, IDX_DIM, N_IDX_HEADS, W_SB, W_SB, W_SB, d_model, d_model)
    offs = np.concatenate([[0], np.cumsum(sizes)])
    col = lambda k: w_in[:, :, offs[k]:offs[k + 1]]
    w_pad = jnp.zeros((depth, d_model, 2 * BLK - 3 * HEAD_DIM - N_IDX_HEADS), w_in.dtype)
    w_a = jnp.concatenate([col(0), col(3), col(1), col(2), col(4), col(5), w_pad], axis=-1).astype(MXU_DTYPE)
    w_b = w_in[:, :, offs[6]:offs[9]].astype(MXU_DTYPE)
    w_c = w_in[:, :, offs[9]:offs[11]].astype(MXU_DTYPE)
    seg_np = np.kron(np.eye(N_HEADS_A), np.full((HEAD_DIM, HEAD_DIM), 1.0 / HEAD_DIM))
    seg = jnp.asarray(seg_np, MXU_DTYPE)
    kg_pad = jnp.concatenate([k_norm_g, jnp.ones((depth, 2 * BLK - HEAD_DIM), k_norm_g.dtype)], axis=-1)
    rbt = rel_bias.T.astype(F32)
    bidx = jnp.asarray(_near_bucket_table())

    layers = []
    for l in range(depth):
        layers.append(dict(
            ln_mix_g=ln_mix_g[l][None], w_a=w_a[l], w_b=w_b[l], w_c=w_c[l],
            qg=jnp.tile(q_norm_g[l], N_HEADS_A)[None], kg=kg_pad[l][None], seg=seg,
            w_pa=w_proj_a[l].astype(MXU_DTYPE), w_ps=w_proj_sb[l].astype(MXU_DTYPE), w_o=w_out[l].astype(MXU_DTYPE),
            ln_ffn_g=ln_ffn_g[l][None], w_up_a=w_up[l][:, :d_ff].astype(MXU_DTYPE),
            w_up_u=w_up[l][:, d_ff:].astype(MXU_DTYPE), conv_w=conv_w[l], conv_b=conv_b[l][None],
            w_down=w_down[l].astype(MXU_DTYPE)))

    np_pad = -(-n_p // BLK) * BLK
    meta = jnp.broadcast_to(meta_tokens.astype(x_prompt.dtype)[None], (b_p, n_meta, d_model))
    xp = jnp.concatenate([meta, x_prompt, jnp.zeros((b_p, np_pad - n_p, d_model), x_prompt.dtype)], axis=1)
    zero_state = jnp.zeros((b_p, SUBLANES, d_ff), F32)

    ns_pad = -(-n_s // BLK) * BLK
    assert past % BLK == 0 and ns_pad == BLK
    xs = jnp.pad(x_sample, ((0, 0), (0, ns_pad - n_s), (0, 0)))

    outs_p, outs_s = [], []
    for l in range(depth):
        lw = layers[l]
        xp, rows_p = _layer(xp, None, zero_state, lw, n_new=n_p, qb0=0, n_valid=n_p, chunk_off=n_meta,
                            topk=topk_p, rbt=rbt, bidx=bidx)
        outs_p.append(rows_p)

        prefix = (cache_a_k[l].reshape(b_s, past, HEAD_DIM).astype(MXU_DTYPE),
                  cache_a_v[l].reshape(b_s, past, HEAD_DIM).astype(MXU_DTYPE),
                  cache_idx_k[l].astype(MXU_DTYPE),
                  cache_sb_k[l].reshape(b_s, past, W_SB).astype(MXU_DTYPE),
                  cache_sb_v[l].reshape(b_s, past, W_SB).astype(MXU_DTYPE))
        st = jnp.pad(state_ffn_conv[l].astype(F32), ((0, 0), (SUBLANES - (CONV_W - 1), 0), (0, 0)))
        xs, rows_s = _layer(xs, prefix, st, lw, n_new=n_s, qb0=past // BLK, n_valid=past + n_s, chunk_off=0,
                            topk=topk_s, rbt=rbt, bidx=bidx)
        outs_s.append(rows_s)

    def group(outs, n):
        stack = lambda name: jnp.stack([o[name] for o in outs])
        kvk = stack("kvk")[:, :, :n]
        cut = lambda lo, width, tail: kvk[..., lo:lo + width].reshape(kvk.shape[:3] + tail)
        heads = lambda name: stack(name)[:, :, :n].reshape(kvk.shape[:3] + (N_HEADS_SB, SB_HEAD_DIM))
        return (cut(0, HEAD_DIM, (1, HEAD_DIM)), cut(HEAD_DIM, HEAD_DIM, (1, HEAD_DIM)),
                cut(2 * HEAD_DIM, IDX_DIM, (IDX_DIM,)), heads("sb_k"), heads("sb_v"), stack("conv"))

    y_prompt = xp[:, n_meta:n_p]
    y_sample = xs[:, :n_s]
    return (y_prompt, y_sample) + group(outs_p, n_p) + group(outs_s, n_s)
```

```python
import functools
import math

import numpy as np
import jax
import jax.numpy as jnp
from jax import lax
from jax.experimental import pallas as pl
from jax.experimental.pallas import tpu as pltpu

CHUNK = 64
HEAD_DIM = 64
N_HEADS_A = 8
N_IDX_HEADS = 4
IDX_DIM = 64
TOPK_MAX = 256
N_HEADS_SB = 4
SB_HEAD_DIM = 128
N_BUCKETS = 32
MAX_DISTANCE = 128
CONV_W = 3
EPS = 1e-6

W_A = N_HEADS_A * HEAD_DIM
W_IDX_Q = N_IDX_HEADS * IDX_DIM
W_SB = N_HEADS_SB * SB_HEAD_DIM

BLK = 128
DSA_UNROLL = 4
SCORE_UNROLL = 2 * DSA_UNROLL
SB_Q_BLOCKS = 2
DENOM_ROWS = 16
LOG2E = math.log2(math.e)
SUBLANES = 8
ROW_TILE_MAX = 640
VMEM_LIMIT = 56 * 1024 * 1024
MXU_DTYPE = jnp.bfloat16
INT_MIN = -2 ** 31
SB_UNDERFLOW = -104.0

F32 = jnp.float32


def _cparams(sem):
    return pltpu.CompilerParams(dimension_semantics=sem, vmem_limit_bytes=VMEM_LIMIT)


def _row_tile(n_rows, cap=ROW_TILE_MAX):
    if n_rows <= BLK:
        return n_rows
    best = BLK
    t = BLK
    while t <= min(cap, n_rows):
        if n_rows % t == 0:
            best = t
        t += BLK
    return best


def _rms(x, g):
    return x * lax.rsqrt(jnp.mean(x * x, axis=-1, keepdims=True) + EPS) * g


def _dot(a, b):
    return jnp.dot(a, b, preferred_element_type=F32)


def _split_dot(a, b):
    hi = a.astype(MXU_DTYPE)
    lo = (a - hi.astype(F32)).astype(MXU_DTYPE)
    return _dot(hi, b) + _dot(lo, b)


def _proj_a_kernel(x_ref, g_ref, w_ref, qg_ref, kg_ref, seg_ref,
                   qa_ref, qi_ref, kvk_ref, ka_ref, va_ref, ki_ref):
    xn = _rms(x_ref[...], g_ref[...])
    y = _dot(xn.astype(MXU_DTYPE), w_ref[...])
    q = y[:, :W_A]
    ms = _split_dot(q * q, seg_ref[...])
    qa_ref[...] = (q * lax.rsqrt(ms + EPS) * qg_ref[...]).astype(qa_ref.dtype)
    qi_ref[...] = y[:, W_A:W_A + W_IDX_Q].astype(qi_ref.dtype)
    kvk = y[:, W_A + W_IDX_Q:]
    lane = lax.broadcasted_iota(jnp.int32, kvk.shape, 1)
    is_k = lane < HEAD_DIM
    msk = jnp.sum(jnp.where(is_k, kvk * kvk, 0.0), axis=-1, keepdims=True) * (1.0 / HEAD_DIM)
    kvk = jnp.where(is_k, kvk * lax.rsqrt(msk + EPS) * kg_ref[...], kvk)
    kvk_ref[...] = kvk
    ka_ref[...] = kvk[:, 0:HEAD_DIM].astype(ka_ref.dtype)
    va_ref[...] = kvk[:, HEAD_DIM:2 * HEAD_DIM].astype(va_ref.dtype)
    ki_ref[...] = kvk[:, 2 * HEAD_DIM:2 * HEAD_DIM + IDX_DIM].astype(ki_ref.dtype)


def _proj_b_kernel(x_ref, g_ref, w_ref, qs_ref, ks_ref, vs_ref, ksb_ref, vsb_ref):
    xn = _rms(x_ref[...], g_ref[...])
    y = _dot(xn.astype(MXU_DTYPE), w_ref[...])
    qs_ref[...] = y[:, :W_SB].astype(qs_ref.dtype)
    k = y[:, W_SB:2 * W_SB]
    v = y[:, 2 * W_SB:]
    ks_ref[...] = k
    vs_ref[...] = v
    ksb_ref[...] = k.astype(ksb_ref.dtype)
    vsb_ref[...] = v.astype(vsb_ref.dtype)


def _proj_c_kernel(x_ref, g_ref, w_ref, o_ref):
    xn = _rms(x_ref[...], g_ref[...])
    o_ref[...] = _dot(xn.astype(MXU_DTYPE), w_ref[...])


def _rowwise_call(kernel, x, consts, out_cols_dtypes, name):
    m, d = x.shape
    tm = _row_tile(m)
    in_specs = [pl.BlockSpec((tm, d), lambda i: (i, 0))]
    for c in consts:
        in_specs.append(pl.BlockSpec(c.shape, lambda i, nd=c.ndim: (0,) * nd))
    out_shape = [jax.ShapeDtypeStruct((m, n), dt) for n, dt in out_cols_dtypes]
    out_specs = [pl.BlockSpec((tm, n), lambda i: (i, 0)) for n, _ in out_cols_dtypes]
    return pl.pallas_call(
        kernel, grid=(m // tm,), in_specs=in_specs, out_specs=out_specs, out_shape=out_shape,
        compiler_params=_cparams(("arbitrary",)), name=name)(x, *consts)


def _t5_bucket_np(rel):
    half = N_BUCKETS // 2
    max_exact = half // 2
    n = np.abs(rel)
    large = np.full(n.shape, max_exact, dtype=np.int64)
    steps = half - max_exact
    for t in range(1, steps + 1):
        lhs = n.astype(object) ** steps
        rhs = (max_exact ** steps) * ((MAX_DISTANCE // max_exact) ** t)
        large = large + (np.array(lhs >= rhs, dtype=bool)).astype(np.int64)
    large = np.minimum(large, half - 1)
    return np.where(rel > 0, half, 0) + np.where(n < max_exact, n, large)


def _near_bucket_table():
    m = np.arange(2 * BLK)
    rows = [_t5_bucket_np(d * BLK + BLK - m) for d in (-1, 0, 1)]
    return np.stack(rows).astype(np.int32)


def _pair_transpose(x, n_pairs):
    cols = []
    for p in range(n_pairs):
        t = x[:, p * BLK:(p + 1) * BLK].T
        cols.append(t[:HEAD_DIM])
        cols.append(t[HEAD_DIM:])
    return jnp.concatenate(cols, axis=1)


def _dsa_kernel(qa_ref, qi_ref, kvk_ref, ka_ref, ki_ref, vat_ref, rbt_ref, bidx_ref, o_ref,
                key_scr, bias_scr, m_scr, acc_scr, tie_scr, lg0_scr, lg1_scr, cm0_scr, cm1_scr,
                *, qb0, nsb_total, n_valid, chunk_off, topk):
    i = pl.program_id(1) + qb0
    nsb = jnp.minimum((i + 2 + DSA_UNROLL - 1) // DSA_UNROLL, nsb_total)
    nh = N_HEADS_A
    sbk = DSA_UNROLL * BLK

    qat = (_pair_transpose(qa_ref[...].astype(F32), nh // 2) * (HEAD_DIM ** -0.5 * LOG2E)).astype(MXU_DTYPE)
    qit = _pair_transpose(qi_ref[...].astype(F32), N_IDX_HEADS // 2).astype(MXU_DTYPE)
    kw_t = kvk_ref[...][:, BLK:2 * BLK].T
    w_scale = (IDX_DIM ** -0.5) * (N_IDX_HEADS ** -0.5)
    w_rows = [kw_t[IDX_DIM + h:IDX_DIM + h + 1, :] * w_scale for h in range(N_IDX_HEADS)]

    qpos = i * BLK + lax.broadcasted_iota(jnp.int32, (1, BLK), 1)
    kend = (((qpos + (CHUNK - chunk_off)) >> 6) << 6) + chunk_off
    kend = jnp.minimum(kend, n_valid)
    krow = lax.broadcasted_iota(jnp.int32, (sbk, BLK), 0)

    @pl.when(jnp.logical_and(pl.program_id(0) == 0, pl.program_id(1) == 0))
    def _():
        rbt = rbt_ref[...]
        far_bucket = N_BUCKETS // 2 - 1
        cfar = rbt[:, far_bucket:far_bucket + 1]
        bias_scr[0] = jnp.zeros((BLK, nh * BLK), F32)
        for d in range(3):
            idx = bidx_ref[d:d + 1, :]
            tab = jnp.zeros((nh, 2 * BLK), F32)
            for b in range(N_BUCKETS):
                tab = jnp.where(idx == b, rbt[:, b:b + 1], tab)
            tab = (tab - cfar) * LOG2E
            for h in range(nh):
                trow = jnp.broadcast_to(tab[h:h + 1, :], (BLK, 2 * BLK))
                bias_scr[d + 1, :, h * BLK:(h + 1) * BLK] = pltpu.roll(trow, 0, 1, stride=1, stride_axis=0)[:, BLK:]

    def to_key(v):
        bits = lax.bitcast_convert_type(v, jnp.int32)
        key = bits ^ ((bits >> 31) & 0x7FFFFFFF)
        return jnp.where(v == 0.0, 0, key)

    def from_key(key):
        return lax.bitcast_convert_type(key ^ ((key >> 31) & 0x7FFFFFFF), F32)

    sck = SCORE_UNROLL * BLK
    srow = lax.broadcasted_iota(jnp.int32, (sck, BLK), 0)

    def score_block(s_, carry):
        smin, smax = carry
        off = pl.multiple_of(s_ * sck, sck)
        s = jnp.maximum(_dot(ki_ref[pl.ds(off, sck), :], qit), 0.0)
        sc = s[:, 0:BLK] * w_rows[0]
        for h in range(1, N_IDX_HEADS):
            sc = sc + s[:, h * BLK:(h + 1) * BLK] * w_rows[h]
        adm = (srow + off) < kend
        smin = jnp.minimum(smin, jnp.min(jnp.where(adm, sc, jnp.inf), axis=0, keepdims=True))
        smax = jnp.maximum(smax, jnp.max(jnp.where(adm, sc, -jnp.inf), axis=0, keepdims=True))
        key_scr[pl.ds(off, sck), :] = jnp.where(adm, to_key(sc), INT_MIN)
        return smin, smax

    n_score = (nsb * sbk + sck - 1) // sck
    smin, smax = lax.fori_loop(0, n_score, score_block,
                               (jnp.full((1, BLK), jnp.inf, F32), jnp.full((1, BLK), -jnp.inf, F32)))

    n_acc = 4 * SUBLANES

    def count(thrs):
        def body(s_, accs):
            off = pl.multiple_of(s_ * sbk, sbk)
            blk = key_scr[pl.ds(off, sbk), :]
            return tuple(acc + jnp.sum(jnp.where(blk >= t, 1.0, 0.0).reshape(sbk // n_acc, n_acc, BLK), axis=0)
                         for acc, t in zip(accs, thrs))
        accs = lax.fori_loop(0, nsb, body, tuple(jnp.zeros((n_acc, BLK), F32) for _ in thrs))
        return [jnp.sum(acc, axis=0, keepdims=True) for acc in accs]

    kf = float(topk)
    n_adm = kend.astype(F32)
    few = n_adm < kf

    def narrow(state, cand, cnt, live):
        lo, c_lo, hi, c_hi = state
        up = cnt >= kf
        take_lo = live & up & (cand > lo)
        take_hi = live & jnp.logical_not(up) & (cand < hi)
        return (jnp.where(take_lo, cand, lo), jnp.where(take_lo, cnt, c_lo),
                jnp.where(take_hi, cand, hi), jnp.where(take_hi, cnt, c_hi))

    def is_open(state):
        lo, c_lo, hi, c_hi = state
        return jnp.logical_not(few) & (c_lo != kf) & (hi - 1 > lo) & (c_lo - c_hi > 2.0)

    state = (to_key(smin), n_adm, to_key(smax) + 1, jnp.zeros((1, BLK), F32))
    zero = jnp.zeros((1, BLK), jnp.int32)
    c_nonneg, c_pos = count([zero, zero + 1])
    all_lanes = jnp.logical_not(few)
    state = narrow(state, zero, c_nonneg, all_lanes)
    state = narrow(state, zero + 1, c_pos, all_lanes)

    group = 2
    max_passes = 4 * 33

    def n_open_of(state):
        return jnp.sum(jnp.where(is_open(state), 1.0, 0.0))

    def search_cond(c):
        t, n_open = c[0], c[1]
        return jnp.logical_and(t < max_passes, n_open > 0.0)

    def search_steps(c):
        t, _, state = c[0], c[1], c[2:]
        for u in range(group):
            lo, _, hi, _ = state
            mid_v = to_key(0.5 * from_key(lo) + 0.5 * from_key(hi - 1))
            mid_k = (lo >> 1) + (hi >> 1) + (lo & hi & 1)
            cand = jnp.where((t + u) % 4 == 3, mid_k, mid_v)
            cand = jnp.minimum(jnp.maximum(cand, lo + 1), hi - 1)
            (cnt,) = count([cand])
            state = narrow(state, cand, cnt, is_open(state))
        return (t + group, n_open_of(state)) + tuple(state)

    res = lax.while_loop(search_cond, search_steps, (jnp.int32(0), n_open_of(state)) + tuple(state))
    lo, c_lo, hi, c_hi = res[2:]
    exact_k = c_lo == kf
    exact_t = jnp.logical_not(exact_k) & jnp.logical_not(hi - 1 > lo)
    pair = jnp.logical_not(few | exact_k | exact_t)

    def largest_below(bound):
        def body(s_, acc):
            off = pl.multiple_of(s_ * sbk, sbk)
            blk = key_scr[pl.ds(off, sbk), :]
            return jnp.maximum(acc, jnp.max(jnp.where(blk < bound, blk, INT_MIN).reshape(sbk // n_acc, n_acc, BLK),
                                            axis=0))
        acc = lax.fori_loop(0, nsb, body, jnp.full((n_acc, BLK), INT_MIN, jnp.int32))
        rows = n_acc
        while rows > SUBLANES:
            rows //= 2
            acc = jnp.maximum(acc[:rows], acc[rows:2 * rows])
        for shift in (4, 2, 1):
            acc = jnp.maximum(acc, pltpu.roll(acc, shift, 0))
        return acc[0:1]

    kth = lax.cond(jnp.sum(jnp.where(pair, 1.0, 0.0)) > 0.0, lambda: largest_below(hi), lambda: lo)
    thr = jnp.where(few, INT_MIN, jnp.where(exact_k, lo - 1, jnp.where(exact_t, lo, kth)))
    n_tie_keep = jnp.where(few | exact_k, 0.0, kf - c_hi)

    m_scr[...] = jnp.full(m_scr.shape, -jnp.inf, F32)
    acc_scr[...] = jnp.zeros(acc_scr.shape, F32)
    tie_scr[...] = jnp.zeros(tie_scr.shape, F32)
    lrow = lax.broadcasted_iota(jnp.int32, (BLK, BLK), 0)
    lcol = lax.broadcasted_iota(jnp.int32, (BLK, BLK), 1)
    lstrict = jnp.where(lcol < lrow, 1.0, 0.0).astype(MXU_DTYPE)

    def stage_a(s_next, lg_ref, cm_ref, with_bias):
        s_ = jnp.minimum(s_next, nsb - 1)
        off = pl.multiple_of(s_ * sbk, sbk)
        masked = jnp.where(s_next < nsb, 0.0, -jnp.inf)
        tie_run = tie_scr[0:1, :]
        selb = []
        for u in range(DSA_UNROLL):
            key = key_scr[pl.ds(off + u * BLK, BLK), :]
            eq = key == thr
            eqf = jnp.where(eq, 1.0, 0.0)
            ties_before = _dot(lstrict, eqf.astype(MXU_DTYPE)) + tie_run
            sel = (key > thr) | (eq & (ties_before < n_tie_keep))
            tie_run = tie_run + jnp.sum(eqf, axis=0, keepdims=True)
            selb.append(jnp.where(sel, masked, -jnp.inf))
        tie_scr[0:1, :] = tie_run
        kab = ka_ref[pl.ds(off, sbk), :]
        tile_idx = [jnp.clip(s_ * DSA_UNROLL + u - i, -2, 1) + 2 for u in range(DSA_UNROLL)]
        for hp in range(nh // 2):
            lg = _dot(kab, qat[:, hp * 2 * BLK:(hp + 1) * 2 * BLK])
            for hh in range(2):
                h = 2 * hp + hh
                cols = slice(h * BLK, (h + 1) * BLK)
                col_max = None
                for u in range(DSA_UNROLL):
                    rows = slice(u * BLK, (u + 1) * BLK)
                    piece = lg[rows, hh * BLK:(hh + 1) * BLK] + selb[u]
                    if with_bias:
                        piece = piece + bias_scr[tile_idx[u], :, cols]
                    lg_ref[rows, cols] = piece
                    part = jnp.max(piece.reshape(BLK // SUBLANES, SUBLANES, BLK), axis=0)
                    col_max = part if col_max is None else jnp.maximum(col_max, part)
                cm_ref[:, cols] = col_max

    def stage_b(s_, lg_ref, cm_ref):
        m_old = m_scr[0:1, :]
        m_new = jnp.maximum(m_old, jnp.max(cm_ref[...], axis=0, keepdims=True))
        m_safe = jnp.where(m_new == -jnp.inf, 0.0, m_new)
        alpha = jnp.exp2(m_old - m_safe)
        p = jnp.exp2(lg_ref[...] - m_safe)
        vat = vat_ref[jnp.minimum(s_, nsb_total - 1)]
        acc_scr[...] = acc_scr[...] * alpha + _dot(vat, p.astype(MXU_DTYPE))
        m_scr[0:1, :] = m_new

    def pair_body(pi, with_bias):
        s_ = 2 * pi
        stage_a(s_ + 1, lg1_scr, cm1_scr, with_bias)
        stage_b(s_, lg0_scr, cm0_scr)
        stage_a(s_ + 2, lg0_scr, cm0_scr, with_bias)
        stage_b(s_ + 1, lg1_scr, cm1_scr)

    def far_pair(pi, carry):
        pair_body(pi, False)
        return carry

    def near_pair(pi, carry):
        pair_body(pi, True)
        return carry

    n_far_sb = jnp.maximum(i - 1, 0) // DSA_UNROLL
    n_far_pairs = jnp.maximum(n_far_sb - 1, 0) // 2
    stage_a(0, lg0_scr, cm0_scr, True)
    lax.fori_loop(0, n_far_pairs, far_pair, 0)
    lax.fori_loop(n_far_pairs, nsb // 2, near_pair, 0)

    @pl.when(nsb % 2 == 1)
    def _():
        stage_b(nsb - 1, lg0_scr, cm0_scr)

    out_t = acc_scr[0:HEAD_DIM, :] / acc_scr[HEAD_DIM:HEAD_DIM + 1, :]
    for p in range(nh // 2):
        pair = jnp.concatenate([out_t[:, (2 * p) * BLK:(2 * p + 1) * BLK],
                                out_t[:, (2 * p + 1) * BLK:(2 * p + 2) * BLK]], axis=0)
        o_ref[:, p * BLK:(p + 1) * BLK] = pair.T.astype(o_ref.dtype)


def _dsa_call(qa, qi, kvk, ka, ki, va, rbt, bidx, *, qb0, n_valid, chunk_off, topk):
    b, nq, _ = qa.shape
    sbk = DSA_UNROLL * BLK
    pad = (-ka.shape[1]) % (SCORE_UNROLL * BLK)
    ka, ki, va = (jnp.pad(a, ((0, 0), (0, pad), (0, 0))) for a in (ka, ki, va))
    nk = ka.shape[1]
    nsb_total = nk // sbk
    vat = va.reshape(b, nsb_total, sbk, HEAD_DIM).transpose(0, 1, 3, 2)
    vat = jnp.concatenate([vat, jnp.ones((b, nsb_total, DENOM_ROWS, sbk), vat.dtype)], axis=2)
    nh = N_HEADS_A
    kernel = functools.partial(_dsa_kernel, qb0=qb0, nsb_total=nsb_total, n_valid=n_valid,
                               chunk_off=chunk_off, topk=topk)
    qspec = lambda w: pl.BlockSpec((None, BLK, w), lambda bi, qi_: (bi, qi_, 0))
    full = lambda a: pl.BlockSpec((None,) + a.shape[1:], lambda bi, qi_, nd=a.ndim: (bi,) + (0,) * (nd - 1))
    const = lambda a: pl.BlockSpec(a.shape, lambda bi, qi_, nd=a.ndim: (0,) * nd)
    return pl.pallas_call(
        kernel, grid=(b, nq // BLK),
        in_specs=[qspec(W_A), qspec(W_IDX_Q), qspec(2 * BLK), full(ka), full(ki), full(vat), const(rbt), const(bidx)],
        out_specs=qspec(W_A),
        out_shape=jax.ShapeDtypeStruct((b, nq, W_A), MXU_DTYPE),
        scratch_shapes=[
            pltpu.VMEM((nk, BLK), jnp.int32),
            pltpu.VMEM((4, BLK, nh * BLK), F32),
            pltpu.VMEM((SUBLANES, nh * BLK), F32),
            pltpu.VMEM((HEAD_DIM + DENOM_ROWS, nh * BLK), F32),
            pltpu.VMEM((SUBLANES, BLK), F32),
            pltpu.VMEM((sbk, nh * BLK), F32),
            pltpu.VMEM((sbk, nh * BLK), F32),
            pltpu.VMEM((SUBLANES, nh * BLK), F32),
            pltpu.VMEM((SUBLANES, nh * BLK), F32),
        ],
        compiler_params=_cparams(("arbitrary", "arbitrary")), name="dsa")(qa, qi, kvk, ka, ki, vat, rbt, bidx)


def _sb_kernel(q_ref, k_ref, v_ref, o_ref, *, qb0, nkb_total):
    bq = q_ref.shape[0]
    nbq = bq // BLK
    i0 = qb0 + pl.program_id(1) * nbq
    j_top = jnp.minimum(i0 + nbq - 1, nkb_total - 1)
    qpos = i0 * BLK + lax.broadcasted_iota(jnp.int32, (bq, BLK), 0)
    kcol = lax.broadcasted_iota(jnp.int32, (bq, BLK), 1)
    trow = lax.broadcasted_iota(jnp.int32, (BLK, BLK), 0)
    tcol = lax.broadcasted_iota(jnp.int32, (BLK, BLK), 1)
    tri = jnp.where(trow > tcol, 1.0, 0.0).astype(MXU_DTYPE)
    scale = SB_HEAD_DIM ** -0.5
    hd = SB_HEAD_DIM
    qs = [q_ref[:, h * hd:(h + 1) * hd] for h in range(N_HEADS_SB)]

    def block(j, later_blocks, acc, masked):
        off = pl.multiple_of(j * BLK, BLK)
        if masked:
            before = (kcol + off) < qpos
        new_lb, new_acc = [], []
        for h in range(N_HEADS_SB):
            kb = k_ref[pl.ds(off, BLK), h * hd:(h + 1) * hd]
            vb = v_ref[pl.ds(off, BLK), h * hd:(h + 1) * hd]
            z = lax.dot_general(qs[h], kb, (((1,), (1,)), ((), ())), preferred_element_type=F32) * scale
            log_keep = -(jnp.maximum(z, 0.0) + jnp.log1p(jnp.exp(-jnp.abs(z))))
            if masked:
                log_keep = jnp.where(before, log_keep, 0.0)
            later = _split_dot(log_keep, tri) + later_blocks[h]
            a = jnp.exp(log_keep + z + later)
            if masked:
                a = jnp.where(before, a, 0.0)
            new_acc.append(acc[h] + _dot(a.astype(MXU_DTYPE), vb))
            new_lb.append(later_blocks[h] + jnp.sum(log_keep, axis=1, keepdims=True))
        return tuple(new_lb), tuple(new_acc)

    def worst(lb):
        return jnp.max(functools.reduce(jnp.maximum, lb))

    later_blocks = tuple(jnp.zeros((bq, 1), F32) for _ in range(N_HEADS_SB))
    acc = tuple(jnp.zeros((bq, hd), F32) for _ in range(N_HEADS_SB))
    for t in range(nbq):
        later_blocks, acc = block(jnp.maximum(j_top - t, 0), later_blocks, acc, True)

    def cond(c):
        j, w, _, _ = c
        return jnp.logical_and(j >= 0, w > SB_UNDERFLOW)

    def body(c):
        j, _, lb, ac = c
        lb, ac = block(j, lb, ac, False)
        return j - 1, worst(lb), lb, ac

    _, _, _, acc = lax.while_loop(cond, body, (j_top - nbq, worst(later_blocks), later_blocks, acc))
    for h in range(N_HEADS_SB):
        o_ref[:, h * hd:(h + 1) * hd] = acc[h].astype(o_ref.dtype)


def _sb_call(q, k, v, *, qb0):
    b, nq, _ = q.shape
    nk = k.shape[1]
    nbq = SB_Q_BLOCKS if nq > BLK else 1
    bq = nbq * BLK
    pad = (-nq) % bq
    if pad:
        q = jnp.pad(q, ((0, 0), (0, pad), (0, 0)))
    kernel = functools.partial(_sb_kernel, qb0=qb0, nkb_total=nk // BLK)
    kv_spec = pl.BlockSpec((None, nk, W_SB), lambda bi, qi_: (bi, 0, 0))
    q_spec = pl.BlockSpec((None, bq, W_SB), lambda bi, qi_: (bi, qi_, 0))
    out = pl.pallas_call(
        kernel, grid=(b, (nq + pad) // bq),
        in_specs=[q_spec, kv_spec, kv_spec], out_specs=q_spec,
        out_shape=jax.ShapeDtypeStruct((b, nq + pad, W_SB), MXU_DTYPE),
        compiler_params=_cparams(("arbitrary", "arbitrary")), name="sb")(q, k, v)
    return out[:, :nq] if pad else out


def _merge_kernel(x_ref, ya_ref, ys_ref, g_ref, wpa_ref, wps_ref, wo_ref, o_ref):
    d = x_ref.shape[-1]
    g = g_ref[...]
    m = (jax.nn.sigmoid(g[:, :d]) * _dot(ya_ref[...], wpa_ref[...])
         + jax.nn.sigmoid(g[:, d:]) * _dot(ys_ref[...], wps_ref[...]))
    o_ref[...] = x_ref[...] + _dot(m.astype(MXU_DTYPE), wo_ref[...])


def _merge_call(x, ya, ys, g, wpa, wps, wo):
    m, d = x.shape
    tm = _row_tile(m)
    row = lambda w: pl.BlockSpec((tm, w), lambda i: (i, 0))
    const = lambda a: pl.BlockSpec(a.shape, lambda i, nd=a.ndim: (0,) * nd)
    return pl.pallas_call(
        _merge_kernel, grid=(m // tm,),
        in_specs=[row(d), row(ya.shape[1]), row(ys.shape[1]), row(g.shape[1]), const(wpa), const(wps), const(wo)],
        out_specs=row(d), out_shape=jax.ShapeDtypeStruct((m, d), F32),
        compiler_params=_cparams(("arbitrary",)), name="merge")(x, ya, ys, g, wpa, wps, wo)


def _ffn_kernel(x_ref, g_ref, st_ref, wa_ref, wu_ref, cw_ref, cb_ref, wd_ref, o_ref,
                h_scr, carry_scr, act_scr):
    t = pl.program_id(1)
    c = pl.program_id(2)
    tm = x_ref.shape[0]

    @pl.when(c == 0)
    def _():
        x = x_ref[...]
        h_scr[...] = _rms(x, g_ref[...]).astype(h_scr.dtype)
        o_ref[...] = x

    @pl.when(t == 0)
    def _():
        carry_scr[c] = st_ref[...]

    h = h_scr[...]
    a = _dot(h, wa_ref[...])
    u = _dot(h, wu_ref[...])
    w0, w1, w2 = cw_ref[0:1, :], cw_ref[1:2, :], cw_ref[2:3, :]
    bias = cb_ref[...]

    def gate(a2, a1, a0, uu):
        cv = bias + (w0 * a2 + w1 * a1 + w2 * a0)
        return cv * jax.nn.sigmoid(cv) * uu

    act_scr[...] = gate(pltpu.roll(a, 2, 0), pltpu.roll(a, 1, 0), a, u)
    prev = carry_scr[c]
    p2 = prev[SUBLANES - 2:SUBLANES - 1, :]
    p1 = prev[SUBLANES - 1:SUBLANES, :]
    top = a[0:SUBLANES]
    rid = lax.broadcasted_iota(jnp.int32, top.shape, 0)
    a1 = jnp.where(rid == 0, p1, pltpu.roll(top, 1, 0))
    a2 = jnp.where(rid == 0, p2, jnp.where(rid == 1, p1, pltpu.roll(top, 2, 0)))
    act_scr[0:SUBLANES, :] = gate(a2, a1, top, u[0:SUBLANES])
    carry_scr[c] = a[tm - SUBLANES:tm]
    o_ref[...] += _dot(act_scr[...].astype(MXU_DTYPE), wd_ref[...])


def _ffn_call(x, g, state, wa, wu, cw, cb, wd, n_chunks):
    b, n, d = x.shape
    dff = wa.shape[1]
    cwid = dff // n_chunks
    tm = _row_tile(n)
    return pl.pallas_call(
        _ffn_kernel, grid=(b, n // tm, n_chunks),
        in_specs=[
            pl.BlockSpec((None, tm, d), lambda bi, t, c: (bi, t, 0)),
            pl.BlockSpec((1, d), lambda bi, t, c: (0, 0)),
            pl.BlockSpec((None, SUBLANES, cwid), lambda bi, t, c: (bi, 0, c)),
            pl.BlockSpec((d, cwid), lambda bi, t, c: (0, c)),
            pl.BlockSpec((d, cwid), lambda bi, t, c: (0, c)),
            pl.BlockSpec((CONV_W, cwid), lambda bi, t, c: (0, c)),
            pl.BlockSpec((1, cwid), lambda bi, t, c: (0, c)),
            pl.BlockSpec((cwid, d), lambda bi, t, c: (c, 0)),
        ],
        out_specs=pl.BlockSpec((None, tm, d), lambda bi, t, c: (bi, t, 0)),
        out_shape=jax.ShapeDtypeStruct((b, n, d), F32),
        scratch_shapes=[
            pltpu.VMEM((tm, d), MXU_DTYPE),
            pltpu.VMEM((n_chunks, SUBLANES, cwid), F32),
            pltpu.VMEM((tm, cwid), F32),
        ],
        compiler_params=_cparams(("arbitrary", "arbitrary", "arbitrary")), name="ffn")(
            x, g, state, wa, wu, cw, cb, wd)


def _ffn_chunks(dff):
    return 2 if dff % (2 * BLK) == 0 else 1


def _layer(x, kv_prefix, conv_state, lw, *, n_new, qb0, n_valid, chunk_off, topk, rbt, bidx):
    b, nq, d = x.shape
    xf = x.reshape(b * nq, d)
    g_mix = lw["ln_mix_g"]
    qa, qi, kvk, ka, va, ki = _rowwise_call(
        _proj_a_kernel, xf, [g_mix, lw["w_a"], lw["qg"], lw["kg"], lw["seg"]],
        [(W_A, MXU_DTYPE), (W_IDX_Q, MXU_DTYPE), (2 * BLK, F32), (HEAD_DIM, MXU_DTYPE), (HEAD_DIM, MXU_DTYPE),
         (IDX_DIM, MXU_DTYPE)], "proj_a")
    qs, ks, vs, ksb, vsb = _rowwise_call(
        _proj_b_kernel, xf, [g_mix, lw["w_b"]],
        [(W_SB, MXU_DTYPE), (W_SB, F32), (W_SB, F32), (W_SB, MXU_DTYPE), (W_SB, MXU_DTYPE)], "proj_b")
    (gates,) = _rowwise_call(_proj_c_kernel, xf, [g_mix, lw["w_c"]], [(2 * d, F32)], "proj_c")

    r3 = lambda a: a.reshape(b, nq, a.shape[-1])
    qa, qi, kvk, ka, va, ki, qs, ks, vs, ksb, vsb = map(r3, (qa, qi, kvk, ka, va, ki, qs, ks, vs, ksb, vsb))

    if kv_prefix is None:
        ka_all, va_all, ki_all, ks_all, vs_all = ka, va, ki, ksb, vsb
    else:
        def join(prefix, new):
            cat = jnp.concatenate([prefix, new[:, :n_new]], axis=1)
            pad = (-cat.shape[1]) % BLK
            return jnp.pad(cat, ((0, 0), (0, pad), (0, 0)))
        ka_all, va_all, ki_all, ks_all, vs_all = (join(p, n_) for p, n_ in zip(kv_prefix, (ka, va, ki, ksb, vsb)))
    ya = _dsa_call(qa, qi, kvk, ka_all, ki_all, va_all, rbt, bidx,
                   qb0=qb0, n_valid=n_valid, chunk_off=chunk_off, topk=topk)
    ys = _sb_call(qs, ks_all, vs_all, qb0=qb0)
    x_mid = _merge_call(xf, ya.reshape(b * nq, W_A), ys.reshape(b * nq, W_SB), gates,
                        lw["w_pa"], lw["w_ps"], lw["w_o"]).reshape(b, nq, d)

    last = x_mid[:, n_new - (CONV_W - 1):n_new].reshape(b * (CONV_W - 1), d)
    (conv_rows,) = _rowwise_call(_proj_c_kernel, last, [lw["ln_ffn_g"], lw["w_up_a"]],
                                 [(lw["w_up_a"].shape[1], F32)], "conv_state")
    x_out = _ffn_call(x_mid, lw["ln_ffn_g"], conv_state, lw["w_up_a"], lw["w_up_u"], lw["conv_w"], lw["conv_b"],
                      lw["w_down"], _ffn_chunks(lw["w_up_a"].shape[1]))
    new_rows = dict(kvk=kvk, sb_k=ks, sb_v=vs, conv=conv_rows.reshape(b, CONV_W - 1, -1))
    return x_out, new_rows


def kernel(x_prompt, x_sample, cache_a_k, cache_a_v, cache_idx_k, cache_sb_k, cache_sb_v, state_ffn_conv, meta_tokens, rel_bias, ln_mix_g, w_in, q_norm_g, k_norm_g, w_proj_a, w_proj_sb, w_out, ln_ffn_g, w_up, conv_w, conv_b, w_down):
    depth, d_model, _ = w_in.shape
    b_p, seq, _ = x_prompt.shape
    n_meta = meta_tokens.shape[0]
    b_s, n_s, _ = x_sample.shape
    past = cache_a_k.shape[2]
    d_ff = w_down.shape[1]
    n_p = n_meta + seq
    topk_p = min(TOPK_MAX, seq // 4)
    topk_s = min(TOPK_MAX, (past + n_s) // 4)
    assert n_meta <= CHUNK and CONV_W - 1 <= min(n_s, SUBLANES)

    sizes = (W_A, HEAD_DIM, HEAD_DIM, W_IDX_Q, IDX_DIM, N_IDX_HEADS, W_SB, W_SB, W_SB, d_model, d_model)
    offs = np.concatenate([[0], np.cumsum(sizes)])
    col = lambda k: w_in[:, :, offs[k]:offs[k + 1]]
    w_pad = jnp.zeros((depth, d_model, 2 * BLK - 3 * HEAD_DIM - N_IDX_HEADS), w_in.dtype)
    w_a = jnp.concatenate([col(0), col(3), col(1), col(2), col(4), col(5), w_pad], axis=-1).astype(MXU_DTYPE)
    w_b = w_in[:, :, offs[6]:offs[9]].astype(MXU_DTYPE)
    w_c = w_in[:, :, offs[9]:offs[11]].astype(MXU_DTYPE)
    seg_np = np.kron(np.eye(N_HEADS_A), np.full((HEAD_DIM, HEAD_DIM), 1.0 / HEAD_DIM))
    seg = jnp.asarray(seg_np, MXU_DTYPE)
    kg_pad = jnp.concatenate([k_norm_g, jnp.ones((depth, 2 * BLK - HEAD_DIM), k_norm_g.dtype)], axis=-1)
    rbt = rel_bias.T.astype(F32)
    bidx = jnp.asarray(_near_bucket_table())

    layers = []
    for l in range(depth):
        layers.append(dict(
            ln_mix_g=ln_mix_g[l][None], w_a=w_a[l], w_b=w_b[l], w_c=w_c[l],
            qg=jnp.tile(q_norm_g[l], N_HEADS_A)[None], kg=kg_pad[l][None], seg=seg,
            w_pa=w_proj_a[l].astype(MXU_DTYPE), w_ps=w_proj_sb[l].astype(MXU_DTYPE), w_o=w_out[l].astype(MXU_DTYPE),
            ln_ffn_g=ln_ffn_g[l][None], w_up_a=w_up[l][:, :d_ff].astype(MXU_DTYPE),
            w_up_u=w_up[l][:, d_ff:].astype(MXU_DTYPE), conv_w=conv_w[l], conv_b=conv_b[l][None],
            w_down=w_down[l].astype(MXU_DTYPE)))

    np_pad = -(-n_p // BLK) * BLK
    meta = jnp.broadcast_to(meta_tokens.astype(x_prompt.dtype)[None], (b_p, n_meta, d_model))
    xp = jnp.concatenate([meta, x_prompt, jnp.zeros((b_p, np_pad - n_p, d_model), x_prompt.dtype)], axis=1)
    zero_state = jnp.zeros((b_p, SUBLANES, d_ff), F32)

    ns_pad = -(-n_s // BLK) * BLK
    assert past % BLK == 0 and ns_pad == BLK
    xs = jnp.pad(x_sample, ((0, 0), (0, ns_pad - n_s), (0, 0)))

    outs_p, outs_s = [], []
    for l in range(depth):
        lw = layers[l]
        xp, rows_p = _layer(xp, None, zero_state, lw, n_new=n_p, qb0=0, n_valid=n_p, chunk_off=n_meta,
                            topk=topk_p, rbt=rbt, bidx=bidx)
        outs_p.append(rows_p)

        prefix = (cache_a_k[l].reshape(b_s, past, HEAD_DIM).astype(MXU_DTYPE),
                  cache_a_v[l].reshape(b_s, past, HEAD_DIM).astype(MXU_DTYPE),
                  cache_idx_k[l].astype(MXU_DTYPE),
                  cache_sb_k[l].reshape(b_s, past, W_SB).astype(MXU_DTYPE),
                  cache_sb_v[l].reshape(b_s, past, W_SB).astype(MXU_DTYPE))
        st = jnp.pad(state_ffn_conv[l].astype(F32), ((0, 0), (SUBLANES - (CONV_W - 1), 0), (0, 0)))
        xs, rows_s = _layer(xs, prefix, st, lw, n_new=n_s, qb0=past // BLK, n_valid=past + n_s, chunk_off=0,
                            topk=topk_s, rbt=rbt, bidx=bidx)
        outs_s.append(rows_s)

    def group(outs, n):
        stack = lambda name: jnp.stack([o[name] for o in outs])
        kvk = stack("kvk")[:, :, :n]
        cut = lambda lo, width, tail: kvk[..., lo:lo + width].reshape(kvk.shape[:3] + tail)
        heads = lambda name: stack(name)[:, :, :n].reshape(kvk.shape[:3] + (N_HEADS_SB, SB_HEAD_DIM))
        return (cut(0, HEAD_DIM, (1, HEAD_DIM)), cut(HEAD_DIM, HEAD_DIM, (1, HEAD_DIM)),
                cut(2 * HEAD_DIM, IDX_DIM, (IDX_DIM,)), heads("sb_k"), heads("sb_v"), stack("conv"))

    y_prompt = xp[:, n_meta:n_p]
    y_sample = xs[:, :n_s]
    return (y_prompt, y_sample) + group(outs_p, n_p) + group(outs_s, n_s)
```
